```python
import math
import functools
import jax
import jax.numpy as jnp
from jax import lax
import numpy as np

D_MODEL = 1024
BATCH = 8
SEQ = 2048
DEPTH = 2
DEC_BATCH = 32
DEC_SEQ = 8
PAST_LEN = 16384
PAGE_SIZE = 128

HEAD_DIM = 64
A_WIDTH = D_MODEL // 4
B_WIDTH = D_MODEL // 2
C_WIDTH = D_MODEL - A_WIDTH - B_WIDTH
A_HEADS = A_WIDTH // HEAD_DIM
B_HEADS = B_WIDTH // HEAD_DIM
B_KV_HEADS = B_HEADS // 2
C_HEADS = C_WIDTH // HEAD_DIM
MIX_WIDTH = A_WIDTH + B_WIDTH + C_WIDTH
IDX_HEADS = 8
IDX_DIM = 64
TOPK_MAX = 256
Q_BLOCK = 128
ROT_DIM = HEAD_DIM // 4
ROPE_THETA = 500000.0
MLSTM_CHUNK = 64
DECAY_LORA = 64
AAA_LORA = 64
GATE_LORA = 128
RWKV_LN_EPS = 64e-5
NORM_EPS = 1e-6
D_FF = ((8 * D_MODEL + 767) // 768) * 256
A_SPLITS = (A_WIDTH, A_WIDTH, A_WIDTH, A_WIDTH, 2 * A_HEADS)
B_SPLITS = (B_WIDTH, B_KV_HEADS * HEAD_DIM, B_KV_HEADS * HEAD_DIM,
            IDX_HEADS * IDX_DIM, IDX_DIM, IDX_HEADS)
C_SPLITS = (C_WIDTH, C_WIDTH, C_WIDTH, DECAY_LORA, AAA_LORA, GATE_LORA)
A_PROJ = sum(A_SPLITS)
B_PROJ = sum(B_SPLITS)
C_PROJ = sum(C_SPLITS)
P_IN = A_PROJ + B_PROJ + C_PROJ

kernel_name = 'hybrid_mlstm_dsa_rwkv7_decode_step'


def split_cols(a, sizes):
    cuts = [sum(sizes[:i + 1]) for i in range(len(sizes) - 1)]
    return jnp.split(a, cuts, axis=-1)


def rmsnorm(x, g):
    xf = x.astype(jnp.float32)
    y = xf * lax.rsqrt(jnp.mean(xf * xf, axis=-1, keepdims=True) + NORM_EPS)
    return (y * g.astype(jnp.float32)).astype(x.dtype)


def rotary(x, pos):
    half = ROT_DIM // 2
    inv = ROPE_THETA ** (-jnp.arange(half, dtype=jnp.float32) / half)
    ang = pos.astype(jnp.float32)[:, None] * inv[None, :]
    cos = jnp.cos(ang)[None, :, None, :]
    sin = jnp.sin(ang)[None, :, None, :]
    xr = x[..., :ROT_DIM].astype(jnp.float32)
    x1, x2 = xr[..., :half], xr[..., half:]
    rot = jnp.concatenate([x1 * cos - x2 * sin, x2 * cos + x1 * sin], axis=-1).astype(x.dtype)
    return jnp.concatenate([rot, x[..., ROT_DIM:]], axis=-1)


def mlstm_chunkwise(q, k, v, log_i, log_f, C0, n0, m0):
    Bsz, L, H, d = q.shape
    Lc = math.gcd(L, MLSTM_CHUNK)
    nc = L // Lc

    def chunks(a):
        a = a.reshape(Bsz, nc, Lc, H, *a.shape[3:])
        return jnp.moveaxis(jnp.moveaxis(a, 1, 0), 3, 2)

    causal = jnp.tril(jnp.ones((Lc, Lc), dtype=bool))

    def step(carry, inp):
        C, n, m = carry
        qc, kc, vc, ic, fc = inp
        b = jnp.cumsum(fc, axis=-1)
        D = jnp.where(causal, b[..., :, None] - b[..., None, :] + ic[..., None, :], -jnp.inf)
        inter = b + m[..., None]
        m_t = jnp.maximum(inter, jnp.max(D, axis=-1))
        s = jnp.einsum('bhtd,bhsd->bhts', qc, kc) * jnp.exp(D - m_t[..., None])
        w_inter = jnp.exp(inter - m_t)
        num = jnp.einsum('bhts,bhsd->bhtd', s, vc) + w_inter[..., None] * jnp.einsum('bhvd,bhtd->bhtv', C, qc)
        den = jnp.sum(s, axis=-1) + w_inter * jnp.einsum('bhd,bhtd->bht', n, qc)
        h = num / jnp.maximum(jnp.abs(den), jnp.exp(-m_t))[..., None]
        b_last = b[..., -1]
        g = b_last[..., None] - b + ic
        m_new = jnp.maximum(b_last + m, jnp.max(g, axis=-1))
        wk = jnp.exp(g - m_new[..., None])
        dec = jnp.exp(b_last + m - m_new)
        C = dec[..., None, None] * C + jnp.einsum('bhs,bhsv,bhsd->bhvd', wk, vc, kc)
        n = dec[..., None] * n + jnp.einsum('bhs,bhsd->bhd', wk, kc)
        return (C, n, m_new), h

    (C, n, m), hs = lax.scan(step, (C0, n0, m0),
                             (chunks(q), chunks(k), chunks(v), chunks(log_i), chunks(log_f)))
    h = jnp.moveaxis(jnp.moveaxis(hs, 2, 3), 0, 1).reshape(Bsz, L, H, d)
    return h, C, n, m


def rwkv7_scan(r, decay, k, v, a, b, S0):
    def step(S, inp):
        rt, dt, kt, vt, at, bt = inp
        sa = jnp.einsum('bhvk,bhk->bhv', S, at)
        S = S * dt[:, :, None, :] + sa[..., None] * bt[:, :, None, :] + vt[..., None] * kt[:, :, None, :]
        return S, jnp.einsum('bhvk,bhk->bhv', S, rt)

    tm = lambda t: jnp.moveaxis(t, 1, 0)
    S, ys = lax.scan(step, S0, (tm(r), tm(decay), tm(k), tm(v), tm(a), tm(b)))
    return jnp.moveaxis(ys, 0, 1), S


def index_scores(iq, iw, ik, mask):
    s = jnp.einsum('bthd,bsd->bths', iq.astype(jnp.float32), ik.astype(jnp.float32))
    s = jax.nn.relu(s * IDX_DIM ** -0.5)
    score = jnp.einsum('bths,bth->bts', s, iw.astype(jnp.float32) * IDX_HEADS ** -0.5)
    return jnp.where(mask[None], score, -jnp.inf)


def sparse_attend(q, k_sel, v_sel, valid):
    Bsz, T, H, d = q.shape
    qg = q.reshape(Bsz, T, B_KV_HEADS, H // B_KV_HEADS, d)
    s = jnp.einsum('bthgd,btkhd->bthgk', qg, k_sel).astype(jnp.float32) * d ** -0.5
    s = jnp.where(valid[:, :, None, None, :], s, -jnp.inf)
    pr = jax.nn.softmax(s, axis=-1).astype(v_sel.dtype)
    o = jnp.einsum('bthgk,btkhd->bthgd', pr, v_sel)
    return o.reshape(Bsz, T, H, d)


def prompt_dsa(q, k, v, iq, ik, iw):
    Bsz, T = q.shape[:2]
    n_sel = min(TOPK_MAX, T // 4)
    qb_len = min(Q_BLOCK, T)
    nb = T // qb_len
    key_pos = jnp.arange(T, dtype=jnp.int32)

    def blk(args):
        qb, iqb, iwb, qpos = args
        sc = index_scores(iqb, iwb, ik, key_pos[None, :] <= qpos[:, None])
        vals, idx = lax.top_k(sc, n_sel)
        k_sel = jax.vmap(lambda kb, ib: kb[ib])(k, idx)
        v_sel = jax.vmap(lambda vb, ib: vb[ib])(v, idx)
        return sparse_attend(qb, k_sel, v_sel, jnp.isfinite(vals))

    to_blocks = lambda a: jnp.moveaxis(a.reshape(Bsz, nb, qb_len, *a.shape[2:]), 1, 0)
    out = lax.map(blk, (to_blocks(q), to_blocks(iq), to_blocks(iw), key_pos.reshape(nb, qb_len)))
    return jnp.moveaxis(out, 0, 1).reshape(q.shape)


def sample_dsa(q, k, v, iq, ik, iw, ck, cv, cik, page_table):
    DB, T = q.shape[:2]
    n_pages = page_table.shape[1]
    past = n_pages * PAGE_SIZE
    L = past + T
    n_sel = min(TOPK_MAX, L // 4)
    ik_all = jnp.concatenate([cik[page_table].reshape(DB, past, IDX_DIM).astype(ik.dtype), ik], axis=1)
    q_pos = past + jnp.arange(T, dtype=jnp.int32)
    key_pos = jnp.arange(L, dtype=jnp.int32)
    sc = index_scores(iq, iw, ik_all, key_pos[None, :] <= q_pos[:, None])
    vals, idx = lax.top_k(sc, n_sel)
    bidx = jnp.arange(DB)[:, None, None]
    phys = page_table[bidx, jnp.minimum(idx // PAGE_SIZE, n_pages - 1)]
    off = idx % PAGE_SIZE
    j = jnp.clip(idx - past, 0, T - 1)
    is_new = (idx >= past)[..., None, None]
    k_sel = jnp.where(is_new, k[bidx, j], ck[phys, off].astype(k.dtype))
    v_sel = jnp.where(is_new, v[bidx, j], cv[phys, off].astype(v.dtype))
    return sparse_attend(q, k_sel, v_sel, jnp.isfinite(vals))


def token_mixers(h, l, pos, mstate, shift0, S0, attn_fn, p):
    Bsz, T, _ = h.shape
    f32 = lambda a: a.astype(jnp.float32)
    heads = lambda a, n: a.reshape(Bsz, T, n, -1)
    proj = h @ p['w_in'][l]
    pa, pb, pc = split_cols(proj, (A_PROJ, B_PROJ, C_PROJ))

    aq, ak, av, ao, ag = split_cols(pa, A_SPLITS)
    gates = f32(ag) + f32(p['mlstm_gate_b'][l])
    log_i = gates[..., :A_HEADS]
    log_f = jax.nn.log_sigmoid(gates[..., A_HEADS:])
    C0, n0, m0 = mstate
    h_a, C, n, m = mlstm_chunkwise(f32(heads(aq, A_HEADS)), f32(heads(ak, A_HEADS)) * HEAD_DIM ** -0.5,
                                   f32(heads(av, A_HEADS)), log_i, log_f, f32(C0), f32(n0), f32(m0))
    h_a = rmsnorm(h_a, p['mlstm_norm'][l].reshape(A_HEADS, HEAD_DIM)).astype(h.dtype)
    y_a = jax.nn.sigmoid(ao) * h_a.reshape(Bsz, T, A_WIDTH)

    bq, bk, bv, biq, bik, biw = split_cols(pb, B_SPLITS)
    q = rotary(heads(bq, B_HEADS), pos)
    k = rotary(heads(bk, B_KV_HEADS), pos)
    v = heads(bv, B_KV_HEADS)
    iq = rotary(heads(biq, IDX_HEADS), pos)
    ik = rotary(bik[:, :, None, :], pos)[:, :, 0, :]
    y_b = attn_fn(q, k, v, iq, ik, biw).reshape(Bsz, T, B_WIDTH)

    prev = jnp.concatenate([shift0[:, None, :].astype(pc.dtype), pc[:, :-1]], axis=1)
    pcs = pc + p['rwkv_mu'][l] * (prev - pc)
    cr, ck, cv, cwl, cal, cgl = split_cols(f32(pcs), C_SPLITS)
    w = -jax.nn.softplus(-(p['rwkv_w0'][l] + jnp.tanh(cwl) @ p['rwkv_w2'][l])) - 0.5
    decay = jnp.exp(-jnp.exp(w))
    a = jax.nn.sigmoid(p['rwkv_a0'][l] + cal @ p['rwkv_a2'][l])
    g = jax.nn.sigmoid(cgl) @ p['rwkv_g2'][l]
    kk = heads(ck * p['rwkv_k_k'][l], C_HEADS)
    kk = kk / jnp.maximum(jnp.sqrt(jnp.sum(kk * kk, axis=-1, keepdims=True)), 1e-12)
    ck = ck * (1.0 + (a - 1.0) * p['rwkv_k_a'][l])
    rh, kh, vh, ah, dh = (heads(t, C_HEADS) for t in (cr, ck, cv, a, decay))
    yc, S = rwkv7_scan(rh, dh, kh, vh, -kk, kk * ah, f32(S0))
    mu = jnp.mean(yc, axis=-1, keepdims=True)
    var = jnp.mean(jnp.square(yc - mu), axis=-1, keepdims=True)
    yc = (yc - mu) * lax.rsqrt(var + RWKV_LN_EPS)
    yc = yc * p['rwkv_ln_w'][l].reshape(C_HEADS, HEAD_DIM) + p['rwkv_ln_b'][l].reshape(C_HEADS, HEAD_DIM)
    yc = yc + jnp.sum(rh * kh * p['rwkv_r_k'][l], axis=-1, keepdims=True) * vh
    y_c = (yc.reshape(Bsz, T, C_WIDTH) * g).astype(h.dtype)

    mixed = jnp.concatenate([y_a, y_b, y_c], axis=-1) @ p['w_out'][l]
    return mixed, (k, v, ik, C, n, m, pc[:, -1], S)


def hybrid_layer(x, l, pos, mstate, shift0, S0, attn_fn, p):
    mixed, new = token_mixers(rmsnorm(x, p['norm_mix'][l]), l, pos, mstate, shift0, S0, attn_fn, p)
    x = x + mixed
    hf = rmsnorm(x, p['norm_ffn'][l])
    x = x + (jax.nn.silu(hf @ p['w_gate'][l]) * (hf @ p['w_up'][l])) @ p['w_down'][l]
    return x, new


def setup_inputs(seed: int = 0) -> dict:
    key = jax.random.key(seed)
    ks = list(jax.random.split(key, 40))
    nrm = lambda i, shape, scale: scale * jax.random.normal(ks[i], shape, jnp.float32)
    n_pages = PAST_LEN // PAGE_SIZE
    n_pool = (5 * DEC_BATCH * n_pages) // 4
    page_table = jax.random.permutation(ks[10], n_pool)[:DEC_BATCH * n_pages]
    page_table = page_table.reshape(DEC_BATCH, n_pages).astype(jnp.int32)
    gate_b = jnp.concatenate([nrm(13, (DEPTH, A_HEADS), 0.1), 3.0 + nrm(14, (DEPTH, A_HEADS), 0.5)], axis=-1)
    return {
        'x_prompt': nrm(0, (BATCH, SEQ, D_MODEL), 1.0),
        'x_sample': nrm(1, (DEC_BATCH, DEC_SEQ, D_MODEL), 1.0),
        'cache_k': nrm(2, (DEPTH, n_pool, PAGE_SIZE, B_KV_HEADS, HEAD_DIM), 1.0),
        'cache_v': nrm(3, (DEPTH, n_pool, PAGE_SIZE, B_KV_HEADS, HEAD_DIM), 1.0),
        'cache_idx_k': nrm(4, (DEPTH, n_pool, PAGE_SIZE, IDX_DIM), 1.0),
        'state_mlstm_C': nrm(5, (DEPTH, DEC_BATCH, A_HEADS, HEAD_DIM, HEAD_DIM), 0.1),
        'state_mlstm_n': nrm(6, (DEPTH, DEC_BATCH, A_HEADS, HEAD_DIM), 1.0),
        'state_mlstm_m': nrm(7, (DEPTH, DEC_BATCH, A_HEADS), 0.5),
        'state_rwkv_shift': nrm(8, (DEPTH, DEC_BATCH, C_PROJ), 1.0),
        'state_rwkv_S': nrm(9, (DEPTH, DEC_BATCH, C_HEADS, HEAD_DIM, HEAD_DIM), 0.1),
        'page_table': page_table,
        'norm_mix': 1.0 + nrm(11, (DEPTH, D_MODEL), 0.02),
        'w_in': nrm(12, (DEPTH, D_MODEL, P_IN), D_MODEL ** -0.5),
        'mlstm_gate_b': gate_b,
        'mlstm_norm': 1.0 + nrm(15, (DEPTH, A_WIDTH), 0.02),
        'rwkv_mu': jax.random.uniform(ks[16], (DEPTH, C_PROJ), jnp.float32),
        'rwkv_w0': -2.0 + nrm(17, (DEPTH, C_WIDTH), 0.5),
        'rwkv_w2': nrm(18, (DEPTH, DECAY_LORA, C_WIDTH), 0.1 * DECAY_LORA ** -0.5),
        'rwkv_a0': nrm(19, (DEPTH, C_WIDTH), 0.1),
        'rwkv_a2': nrm(20, (DEPTH, AAA_LORA, C_WIDTH), 0.1 * AAA_LORA ** -0.5),
        'rwkv_g2': nrm(21, (DEPTH, GATE_LORA, C_WIDTH), GATE_LORA ** -0.5),
        'rwkv_k_k': 0.85 + nrm(22, (DEPTH, C_WIDTH), 0.05),
        'rwkv_k_a': 1.0 + nrm(23, (DEPTH, C_WIDTH), 0.05),
        'rwkv_r_k': nrm(24, (DEPTH, C_HEADS, HEAD_DIM), 0.1),
        'rwkv_ln_w': 1.0 + nrm(25, (DEPTH, C_WIDTH), 0.02),
        'rwkv_ln_b': nrm(26, (DEPTH, C_WIDTH), 0.01),
        'w_out': nrm(27, (DEPTH, MIX_WIDTH, D_MODEL), MIX_WIDTH ** -0.5),
        'norm_ffn': 1.0 + nrm(28, (DEPTH, D_MODEL), 0.02),
        'w_gate': nrm(29, (DEPTH, D_MODEL, D_FF), D_MODEL ** -0.5),
        'w_up': nrm(30, (DEPTH, D_MODEL, D_FF), D_MODEL ** -0.5),
        'w_down': nrm(31, (DEPTH, D_FF, D_MODEL), D_FF ** -0.5),
        'norm_final': 1.0 + nrm(32, (D_MODEL,), 0.02),
    }


def reference(x_prompt, x_sample, cache_k, cache_v, cache_idx_k, state_mlstm_C, state_mlstm_n,
              state_mlstm_m, state_rwkv_shift, state_rwkv_S, page_table, norm_mix, w_in, mlstm_gate_b,
              mlstm_norm, rwkv_mu, rwkv_w0, rwkv_w2, rwkv_a0, rwkv_a2, rwkv_g2, rwkv_k_k, rwkv_k_a,
              rwkv_r_k, rwkv_ln_w, rwkv_ln_b, w_out, norm_ffn, w_gate, w_up, w_down, norm_final):
    p = dict(norm_mix=norm_mix, w_in=w_in, mlstm_gate_b=mlstm_gate_b, mlstm_norm=mlstm_norm,
             rwkv_mu=rwkv_mu, rwkv_w0=rwkv_w0, rwkv_w2=rwkv_w2, rwkv_a0=rwkv_a0, rwkv_a2=rwkv_a2,
             rwkv_g2=rwkv_g2, rwkv_k_k=rwkv_k_k, rwkv_k_a=rwkv_k_a, rwkv_r_k=rwkv_r_k,
             rwkv_ln_w=rwkv_ln_w, rwkv_ln_b=rwkv_ln_b, w_out=w_out, norm_ffn=norm_ffn,
             w_gate=w_gate, w_up=w_up, w_down=w_down)
    Bp, Tp = x_prompt.shape[:2]
    Ts = x_sample.shape[1]
    past = page_table.shape[1] * PAGE_SIZE
    pos_p = jnp.arange(Tp, dtype=jnp.int32)
    pos_s = past + jnp.arange(Ts, dtype=jnp.int32)
    zC = jnp.zeros((Bp, A_HEADS, HEAD_DIM, HEAD_DIM), jnp.float32)
    zn = jnp.zeros((Bp, A_HEADS, HEAD_DIM), jnp.float32)
    zm = jnp.zeros((Bp, A_HEADS), jnp.float32)
    zshift = jnp.zeros((Bp, C_PROJ), x_prompt.dtype)
    zS = jnp.zeros((Bp, C_HEADS, HEAD_DIM, HEAD_DIM), jnp.float32)
    xp, xs = x_prompt, x_sample
    new_p, new_s = [], []
    for l in range(DEPTH):
        xp, st = hybrid_layer(xp, l, pos_p, (zC, zn, zm), zshift, zS, prompt_dsa, p)
        new_p.append(st)
        attn_s = functools.partial(sample_dsa, ck=cache_k[l], cv=cache_v[l], cik=cache_idx_k[l],
                                   page_table=page_table)
        xs, st = hybrid_layer(xs, l, pos_s, (state_mlstm_C[l], state_mlstm_n[l], state_mlstm_m[l]),
                              state_rwkv_shift[l], state_rwkv_S[l], attn_s, p)
        new_s.append(st)
    y_prompt = rmsnorm(xp, norm_final)
    y_sample = rmsnorm(xs, norm_final)
    stack = lambda states, i: jnp.stack([st[i] for st in states])
    p_k, p_v, p_idx_k, p_mlstm_C, p_mlstm_n, p_mlstm_m, p_rwkv_shift, p_rwkv_S = [stack(new_p, i) for i in range(8)]
    s_k, s_v, s_idx_k, s_mlstm_C, s_mlstm_n, s_mlstm_m, s_rwkv_shift, s_rwkv_S = [stack(new_s, i) for i in range(8)]
    return (y_prompt, y_sample, p_k, p_v, p_idx_k, p_mlstm_C, p_mlstm_n, p_mlstm_m, p_rwkv_shift, p_rwkv_S,
            s_k, s_v, s_idx_k, s_mlstm_C, s_mlstm_n, s_mlstm_m, s_rwkv_shift, s_rwkv_S)
```

```python
import functools

import jax
import jax.numpy as jnp
from jax import lax
from jax.experimental import pallas as pl
from jax.experimental.pallas import tpu as pltpu

F32 = jnp.float32
BF16 = jnp.bfloat16
I32 = jnp.int32

HEAD_DIM = 64
A_HEADS = 4
B_HEADS = 8
B_KV_HEADS = 4
C_HEADS = 4
IDX_HEADS = 8
IDX_DIM = 64
TOPK = 256
Q_BLOCK = 128
ROT_DIM = 16
ROPE_THETA = 500000.0
MLSTM_CHUNK = 64
RWKV_CHUNK = 64
RWKV_LN_EPS = 64e-5
NORM_EPS = 1e-6
PAGE = 128

A_WIDTH = A_HEADS * HEAD_DIM
B_WIDTH = B_HEADS * HEAD_DIM
KV_WIDTH = B_KV_HEADS * HEAD_DIM
C_WIDTH = C_HEADS * HEAD_DIM
IQ_WIDTH = IDX_HEADS * IDX_DIM
A_PROJ = 4 * A_WIDTH + 2 * A_HEADS
B_PROJ = B_WIDTH + 2 * KV_WIDTH + IQ_WIDTH + IDX_DIM + IDX_HEADS
C_PROJ = 3 * C_WIDTH + 64 + 64 + 128

LANES = 128
A_PAD = 4 * A_WIDTH + LANES
B_PAD = B_PROJ + (LANES - (IDX_DIM + IDX_HEADS))
P_PAD = A_PAD + B_PAD + C_PROJ

VMEM_LIMIT = 56 * 1024 * 1024

INT_MIN = -2147483648
KEY_NEG_INF = -2139095040
KEY_POS_INF = 2139095040
NEG_INF = float("-inf")


def _cparams(*sem):
    return pltpu.CompilerParams(dimension_semantics=sem, vmem_limit_bytes=VMEM_LIMIT)


def _dot(a, b):
    return jnp.dot(a.astype(BF16), b.astype(BF16), preferred_element_type=F32)


def _dot_nt(a, b):
    return lax.dot_general(a.astype(BF16), b.astype(BF16), (((1,), (1,)), ((), ())),
                           preferred_element_type=F32)


def _dot_tn(a, b):
    return lax.dot_general(a.astype(BF16), b.astype(BF16), (((0,), (0,)), ((), ())),
                           preferred_element_type=F32)


def _split2(a):
    hi = a.astype(BF16)
    lo = (a - hi.astype(F32)).astype(BF16)
    return hi, lo


def _split3(a):
    hi = a.astype(BF16)
    r = a - hi.astype(F32)
    mid = r.astype(BF16)
    lo = (r - mid.astype(F32)).astype(BF16)
    return hi, mid, lo


_NN = (((1,), (0,)), ((), ()))
_NT = (((1,), (1,)), ((), ()))
_TN = (((0,), (0,)), ((), ()))


def _dg(a, b, dims):
    return lax.dot_general(a, b, dims, preferred_element_type=F32)


def _dot3(a, b, dims=_NN):
    ah, al = _split2(a)
    bh, bl = _split2(b)
    return _dg(ah, bh, dims) + (_dg(ah, bl, dims) + _dg(al, bh, dims))


def _dot01_left(m01, x):
    m = m01.astype(BF16)
    x1, x2, x3 = _split3(x)
    return _dg(m, x1, _NN) + (_dg(m, x2, _NN) + _dg(m, x3, _NN))


def _dot01_right(x, m01):
    m = m01.astype(BF16)
    x1, x2, x3 = _split3(x)
    return _dg(x1, m, _NN) + (_dg(x2, m, _NN) + _dg(x3, m, _NN))


def _sigmoid(x):
    return 1.0 / (1.0 + jnp.exp(-x))


def _softplus(x):
    return jnp.maximum(x, 0.0) + jnp.log1p(jnp.exp(-jnp.abs(x)))


def _iota(shape, dim):
    return lax.broadcasted_iota(I32, shape, dim)


def _rope_tile(xt, cos, sina, sinb):
    up = pltpu.roll(xt, LANES - ROT_DIM // 2, axis=1)
    dn = pltpu.roll(xt, ROT_DIM // 2, axis=1)
    return xt * cos + up * sina + dn * sinb


def _inproj_kernel(x_ref, g_ref, w_ref, cos_ref, sina_ref, sinb_ref,
                   pa_ref, q_ref, k_ref, v_ref, iq_ref, ikw_ref, pc_ref):
    x = x_ref[...]
    ms = jnp.mean(x * x, axis=-1, keepdims=True)
    h = (x * lax.rsqrt(ms + NORM_EPS) * g_ref[...]).astype(BF16)
    cos = cos_ref[...]
    sina = sina_ref[...]
    sinb = sinb_ref[...]

    def proj(lo, width):
        return jnp.dot(h, w_ref[:, lo:lo + width], preferred_element_type=F32)

    def rope(x2):
        tiles = [_rope_tile(x2[:, j * LANES:(j + 1) * LANES], cos, sina, sinb)
                 for j in range(x2.shape[1] // LANES)]
        return tiles[0] if len(tiles) == 1 else jnp.concatenate(tiles, axis=1)

    pa_ref[...] = proj(0, A_PAD)
    b0 = A_PAD
    q_ref[...] = rope(proj(b0, B_WIDTH))
    k_ref[...] = rope(proj(b0 + B_WIDTH, KV_WIDTH))
    v_ref[...] = proj(b0 + B_WIDTH + KV_WIDTH, KV_WIDTH)
    iq_ref[...] = rope(proj(b0 + B_WIDTH + 2 * KV_WIDTH, IQ_WIDTH))
    ikw = proj(b0 + B_WIDTH + 2 * KV_WIDTH + IQ_WIDTH, LANES)
    lane = _iota(ikw.shape, 1)
    ikw_ref[...] = jnp.where(lane < IDX_DIM, _rope_tile(ikw, cos, sina, sinb), ikw)
    pc_ref[...] = proj(A_PAD + B_PAD, C_PROJ)


def _inproj(x2d, g, w_bf, tabs, tm):
    n, d = x2d.shape
    cos, sina, sinb = tabs
    tab_blocks = cos.shape[0] // tm
    row = lambda i: (i, 0)
    fixed = lambda i: (0, 0)
    tab = lambda i: (i % tab_blocks, 0)
    widths = (A_PAD, B_WIDTH, KV_WIDTH, KV_WIDTH, IQ_WIDTH, LANES, C_PROJ)
    return pl.pallas_call(
        _inproj_kernel,
        grid=(n // tm,),
        in_specs=[pl.BlockSpec((tm, d), row), pl.BlockSpec((1, d), fixed),
                  pl.BlockSpec((d, P_PAD), fixed),
                  pl.BlockSpec((tm, LANES), tab), pl.BlockSpec((tm, LANES), tab),
                  pl.BlockSpec((tm, LANES), tab)],
        out_specs=[pl.BlockSpec((tm, w), row) for w in widths],
        out_shape=[jax.ShapeDtypeStruct((n, w), F32) for w in widths],
        compiler_params=_cparams("parallel"),
    )(x2d, g, w_bf, cos, sina, sinb)


def _mlstm_kernel(pa_ref, c0_ref, n0_ref, m0_ref, gb_ref, gn_ref,
                  y_ref, cout_ref, nout_ref, mout_ref, c_s, n_s, m_s, *, L, Lp):
    c = pl.program_id(1)

    @pl.when(c == 0)
    def _():
        c_s[...] = c0_ref[0]
        n_s[...] = n0_ref[0]
        m_s[...] = m0_ref[0]

    pa = pa_ref[0]
    if Lp > L:
        pa = jnp.concatenate([pa, jnp.zeros((Lp - L, pa.shape[1]), F32)], axis=0)
    row = _iota((Lp, Lp), 0)
    col = _iota((Lp, Lp), 1)
    eye = row == col
    causal = row >= col
    valid = _iota((Lp, LANES), 0) < L
    gates = pa[:, 4 * A_WIDTH:4 * A_WIDTH + LANES] + gb_ref[...]
    log_i = jnp.where(valid, gates, NEG_INF)
    log_f = jnp.where(valid, -_softplus(-gates), 0.0)
    bcum = _dot01_left(causal, log_f)

    def to_row(colvec):
        return jnp.sum(jnp.where(eye, colvec, 0.0), axis=0, keepdims=True)

    for h in range(A_HEADS):
        sl = slice(h * HEAD_DIM, (h + 1) * HEAD_DIM)
        b_col = bcum[:, A_HEADS + h:A_HEADS + h + 1]
        i_col = log_i[:, h:h + 1]
        b_row = to_row(b_col)
        i_row = to_row(i_col)
        m_prev = m_s[0:1, h:h + 1]
        dmat = jnp.where(causal, b_col - b_row + i_row, NEG_INF)
        inter = b_col + m_prev
        m_t = jnp.maximum(inter, jnp.max(dmat, axis=1, keepdims=True))
        qh = pa[:, sl]
        kh = pa[:, A_WIDTH + h * HEAD_DIM:A_WIDTH + (h + 1) * HEAD_DIM] * (HEAD_DIM ** -0.5)
        vh = pa[:, 2 * A_WIDTH + h * HEAD_DIM:2 * A_WIDTH + (h + 1) * HEAD_DIM]
        oh = pa[:, 3 * A_WIDTH + h * HEAD_DIM:3 * A_WIDTH + (h + 1) * HEAD_DIM]
        s = _dot_nt(qh, kh) * jnp.exp(dmat - m_t)
        w_inter = jnp.exp(inter - m_t)
        c_h = c_s[h]
        n_row = n_s[h:h + 1, :]
        num = _dot(s, vh) + w_inter * _dot_nt(qh, c_h)
        den = jnp.sum(s, axis=1, keepdims=True) + w_inter * jnp.sum(qh * n_row, axis=1, keepdims=True)
        hh = num / jnp.maximum(jnp.abs(den), jnp.exp(-m_t))
        b_last = bcum[Lp - 1:Lp, A_HEADS + h:A_HEADS + h + 1]
        g_row = b_last - b_row + i_row
        g_col = b_last - b_col + i_col
        m_new = jnp.maximum(b_last + m_prev, jnp.max(g_row, axis=1, keepdims=True))
        wk_col = jnp.exp(g_col - m_new)
        dec = jnp.exp(b_last + m_prev - m_new)
        c_s[h] = dec * c_h + _dot_tn(vh * wk_col, kh)
        n_s[h:h + 1, :] = dec * n_row + jnp.sum(kh * wk_col, axis=0, keepdims=True)
        m_s[0:1, h:h + 1] = m_new
        ms = jnp.mean(hh * hh, axis=1, keepdims=True)
        hn = hh * lax.rsqrt(ms + NORM_EPS) * gn_ref[:, sl]
        y_ref[0, :, sl] = (_sigmoid(oh) * hn)[:L]

    @pl.when(c == pl.num_programs(1) - 1)
    def _():
        cout_ref[0] = c_s[...]
        nout_ref[0] = n_s[...]
        mout_ref[0] = m_s[...]


def _mlstm(pa3, c0, n0, m0, gate_b, gnorm):
    bsz, t, _ = pa3.shape
    L = min(t, MLSTM_CHUNK)
    Lp = MLSTM_CHUNK
    nc = t // L
    gb = jnp.zeros((1, LANES), F32).at[0, :2 * A_HEADS].set(gate_b)
    kern = functools.partial(_mlstm_kernel, L=L, Lp=Lp)
    st4 = lambda b, c: (b, 0, 0, 0)
    st3 = lambda b, c: (b, 0, 0)
    fixed = lambda b, c: (0, 0)
    y, cn, nn, mn = pl.pallas_call(
        kern,
        grid=(bsz, nc),
        in_specs=[pl.BlockSpec((1, L, A_PAD), lambda b, c: (b, c, 0)),
                  pl.BlockSpec((1, A_HEADS, HEAD_DIM, HEAD_DIM), st4),
                  pl.BlockSpec((1, A_HEADS, HEAD_DIM), st3),
                  pl.BlockSpec((1, 1, A_HEADS), st3),
                  pl.BlockSpec((1, LANES), fixed),
                  pl.BlockSpec((1, A_WIDTH), fixed)],
        out_specs=[pl.BlockSpec((1, L, A_WIDTH), lambda b, c: (b, c, 0)),
                   pl.BlockSpec((1, A_HEADS, HEAD_DIM, HEAD_DIM), st4),
                   pl.BlockSpec((1, A_HEADS, HEAD_DIM), st3),
                   pl.BlockSpec((1, 1, A_HEADS), st3)],
        out_shape=[jax.ShapeDtypeStruct((bsz, t, A_WIDTH), F32),
                   jax.ShapeDtypeStruct((bsz, A_HEADS, HEAD_DIM, HEAD_DIM), F32),
                   jax.ShapeDtypeStruct((bsz, A_HEADS, HEAD_DIM), F32),
                   jax.ShapeDtypeStruct((bsz, 1, A_HEADS), F32)],
        scratch_shapes=[pltpu.VMEM((A_HEADS, HEAD_DIM, HEAD_DIM), F32),
                        pltpu.VMEM((A_HEADS, HEAD_DIM), F32),
                        pltpu.VMEM((1, A_HEADS), F32)],
        compiler_params=_cparams("parallel", "arbitrary"),
    )(pa3, c0, n0, m0.reshape(bsz, 1, A_HEADS), gb, gnorm.reshape(1, A_WIDTH))
    return y, cn, nn, mn.reshape(bsz, A_HEADS)


def _rwkv_kernel(pc_ref, shift0_ref, s0_ref, mu_ref, w0_ref, w2_ref, a0_ref, a2_ref, g2_ref,
                 kk_ref, ka_ref, rk_ref, lnw_ref, lnb_ref,
                 y_ref, shift_out_ref, s_out_ref, s_s, carry_s, *, L, C):
    c = pl.program_id(1)
    W = C_WIDTH
    HC = C_HEADS * C

    @pl.when(c == 0)
    def _():
        carry_s[...] = shift0_ref[0]
        s_s[...] = jnp.zeros((W, W), F32)
        for h in range(C_HEADS):
            sl = slice(h * HEAD_DIM, (h + 1) * HEAD_DIM)
            s_s[sl, sl] = s0_ref[0, h]

    pc = pc_ref[0]
    if C > L:
        pc = jnp.concatenate([pc, jnp.zeros((C - L, pc.shape[1]), F32)], axis=0)
    rowc = _iota((C, C_PROJ), 0)
    prev = jnp.where(rowc == 0, carry_s[...], pltpu.roll(pc, 1, axis=0))
    carry_s[...] = pc[L - 1:L, :]
    pcs = pc + mu_ref[...] * (prev - pc)
    cr = pcs[:, 0:W]
    ck = pcs[:, W:2 * W]
    cv = pcs[:, 2 * W:3 * W]
    cwl = pcs[:, 3 * W:3 * W + 64]
    cal = pcs[:, 3 * W + 64:3 * W + 128]
    cgl = pcs[:, 3 * W + 128:3 * W + 256]
    wlog = -_softplus(-(w0_ref[...] + _dot(jnp.tanh(cwl), w2_ref[...]))) - 0.5
    lw = -jnp.exp(wlog)
    a = _sigmoid(a0_ref[...] + _dot(cal, a2_ref[...]))
    g = _dot(_sigmoid(cgl), g2_ref[...])

    lane_head = _iota((W, W), 0) // HEAD_DIM == _iota((W, W), 1) // HEAD_DIM

    def head_sum(x):
        return _dot01_right(x, lane_head)

    kk = ck * kk_ref[...]
    kkn = kk / jnp.maximum(jnp.sqrt(head_sum(kk * kk)), 1e-12)
    k2 = ck * (1.0 + (a - 1.0) * ka_ref[...])
    alpha = -kkn
    beta = kkn * a
    if C > L:
        vrow = _iota((C, W), 0) < L
        zero = lambda z: jnp.where(vrow, z, 0.0)
        lw, alpha, beta, k2s, cvs, crs = zero(lw), zero(alpha), zero(beta), zero(k2), zero(cv), zero(cr)
    else:
        k2s, cvs, crs = k2, cv, cr

    tri = _iota((C, C), 0) >= _iota((C, C), 1)
    logp = _dot01_left(tri, lw)
    p = jnp.exp(logp)
    pinv = jnp.exp(-logp)
    pprev = jnp.exp(logp - lw)
    p_last = p[C - 1:C, :]
    ab = alpha * pprev
    bb = beta * pinv
    kb = k2s * pinv
    rb = crs * p

    lane_h = _iota((C, W), 1) // HEAD_DIM

    def stack(x):
        return jnp.concatenate([jnp.where(lane_h == h, x, 0.0) for h in range(C_HEADS)], axis=0)

    a_st, r_st, b_st, k_st, v_st = stack(ab), stack(rb), stack(bb), stack(kb), stack(cvs)
    ar = jnp.concatenate([a_st, r_st], axis=0)
    gb = _dot3(ar, b_st, _NT)
    gk = _dot3(ar, k_st, _NT)
    tt = _iota((HC, HC), 0) % C
    ii = _iota((HC, HC), 1) % C
    l_ab = jnp.where(tt > ii, gb[:HC], 0.0)
    l_rb = jnp.where(tt >= ii, gb[HC:], 0.0)
    l_ak = jnp.where(tt > ii, gk[:HC], 0.0)
    l_rk = jnp.where(tt >= ii, gk[HC:], 0.0)

    x = (_iota((HC, HC), 0) == _iota((HC, HC), 1)).astype(F32)
    s = 1
    while s < C:
        lvl = ((tt // s) % 2 == 1) & ((ii // s) % 2 == 0) & (tt // (2 * s) == ii // (2 * s))
        e = jnp.where(lvl, l_ab, 0.0)
        x = x + _dot3(_dot3(x, e), x)
        s *= 2

    a_t = _dot3(x, a_st)
    v_t = _dot3(x, _dot3(l_ak, v_st))
    r_q = r_st + _dot3(l_rb, a_t)
    y0 = _dot3(l_rb, v_t) + _dot3(l_rk, v_st)
    s_old = s_s[...]
    u = _dot3(a_t, s_old, _NT) + v_t
    y_st = _dot3(r_q, s_old, _NT) + y0
    s_new = (s_old + _dot3(u, b_st, _TN) + _dot3(v_st, k_st, _TN)) * p_last
    s_s[...] = s_new
    yc = y_st[0:C]
    for h in range(1, C_HEADS):
        yc = yc + y_st[h * C:(h + 1) * C]

    inv_d = 1.0 / HEAD_DIM
    mean = head_sum(yc) * inv_d
    dlt = yc - mean
    var = head_sum(dlt * dlt) * inv_d
    ycn = dlt * lax.rsqrt(var + RWKV_LN_EPS) * lnw_ref[...] + lnb_ref[...]
    bonus = head_sum(cr * k2 * rk_ref[...]) * cv
    y_ref[0] = ((ycn + bonus) * g)[:L]

    @pl.when(c == pl.num_programs(1) - 1)
    def _():
        shift_out_ref[0] = pc[L - 1:L, :]
        for h in range(C_HEADS):
            sl = slice(h * HEAD_DIM, (h + 1) * HEAD_DIM)
            s_out_ref[0, h] = s_new[sl, sl]


def _rwkv(pc3, shift0, s0, p, l):
    bsz, t, _ = pc3.shape
    C = RWKV_CHUNK
    L = min(t, C)
    nc = t // L
    r1 = lambda a: a.reshape(1, -1)
    params = [r1(p['rwkv_mu'][l]), r1(p['rwkv_w0'][l]), p['rwkv_w2'][l], r1(p['rwkv_a0'][l]),
              p['rwkv_a2'][l], p['rwkv_g2'][l], r1(p['rwkv_k_k'][l]), r1(p['rwkv_k_a'][l]),
              r1(p['rwkv_r_k'][l]), r1(p['rwkv_ln_w'][l]), r1(p['rwkv_ln_b'][l])]
    fixed = lambda b, c: (0, 0)
    pspecs = [pl.BlockSpec(a.shape, fixed) for a in params]
    kern = functools.partial(_rwkv_kernel, L=L, C=C)
    y, shift, s_new = pl.pallas_call(
        kern,
        grid=(bsz, nc),
        in_specs=[pl.BlockSpec((1, L, C_PROJ), lambda b, c: (b, c, 0)),
                  pl.BlockSpec((1, 1, C_PROJ), lambda b, c: (b, 0, 0)),
                  pl.BlockSpec((1, C_HEADS, HEAD_DIM, HEAD_DIM), lambda b, c: (b, 0, 0, 0))] + pspecs,
        out_specs=[pl.BlockSpec((1, L, C_WIDTH), lambda b, c: (b, c, 0)),
                   pl.BlockSpec((1, 1, C_PROJ), lambda b, c: (b, 0, 0)),
                   pl.BlockSpec((1, C_HEADS, HEAD_DIM, HEAD_DIM), lambda b, c: (b, 0, 0, 0))],
        out_shape=[jax.ShapeDtypeStruct((bsz, t, C_WIDTH), F32),
                   jax.ShapeDtypeStruct((bsz, 1, C_PROJ), F32),
                   jax.ShapeDtypeStruct((bsz, C_HEADS, HEAD_DIM, HEAD_DIM), F32)],
        scratch_shapes=[pltpu.VMEM((C_WIDTH, C_WIDTH), F32), pltpu.VMEM((1, C_PROJ), F32)],
        compiler_params=_cparams("parallel", "arbitrary"),
    )(pc3, shift0.reshape(bsz, 1, C_PROJ), s0, *params)
    return y, shift.reshape(bsz, C_PROJ), s_new


def _order_key(score):
    bits = pltpu.bitcast(score, I32)
    return jnp.where(bits < 0, (bits ^ 0x7FFFFFFF) + 1, bits)


KC = 256


def _dsa_prompt_kernel(q_ref, iq_ref, ikwq_ref, ikw_ref, k_ref, v_ref, y_ref,
                       kbf_s, vt_s, ikbf_s, key_s, bias_s, sc_s, ot_s, *, T):
    qb = pl.program_id(1)
    nkc = T // KC

    @pl.when(qb == 0)
    def _():
        kbf_s[...] = k_ref[0].astype(BF16)
        ikbf_s[...] = ikw_ref[0][:, :IDX_DIM].astype(BF16)
        for cc in range(nkc):
            vt_s[cc] = v_ref[0, cc * KC:(cc + 1) * KC, :].T.astype(BF16)

    nch = (qb + 2) // 2
    iq_t = iq_ref[0].T.astype(BF16)
    w_t = ikwq_ref[0].T[IDX_DIM:IDX_DIM + IDX_HEADS, :] * (IDX_DIM ** -0.5 * IDX_HEADS ** -0.5)
    q_pos = qb * Q_BLOCK + _iota((KC, Q_BLOCK), 1)
    rel = _iota((KC, Q_BLOCK), 0)

    def chunk(cc):
        return pl.ds(pl.multiple_of(cc * KC, KC), KC)

    def score_body(cc, carry):
        ikc = ikbf_s[chunk(cc), :]
        acc = jnp.zeros((KC, Q_BLOCK), F32)
        for h in range(IDX_HEADS):
            st = jnp.dot(ikc, iq_t[h * IDX_DIM:(h + 1) * IDX_DIM, :], preferred_element_type=F32)
            acc = acc + jnp.maximum(st, 0.0) * w_t[h:h + 1, :]
        score = jnp.where(cc * KC + rel <= q_pos, acc, NEG_INF)
        key_s[chunk(cc), :] = _order_key(score)
        return carry

    lax.fori_loop(0, nch, score_body, 0)

    def count(pred):
        def body(cc, acc):
            m = jnp.where(pred(key_s[chunk(cc), :]), 1.0, 0.0)
            return acc + jnp.sum(m.reshape(KC // 8, 8, Q_BLOCK), axis=0)
        part = lax.fori_loop(0, nch, body, jnp.zeros((8, Q_BLOCK), F32))
        return jnp.sum(part, axis=0, keepdims=True)

    def bit_body(i, t):
        cand = t + jnp.left_shift(jnp.int32(1), 31 - i)
        cnt = count(lambda kk: kk >= cand)
        return jnp.where(cnt >= TOPK, cand, t)

    thr = lax.fori_loop(0, 32, bit_body, jnp.full((1, Q_BLOCK), INT_MIN, I32))
    need = TOPK - count(lambda kk: kk > thr)

    tri = (_iota((KC, KC), 0) >= _iota((KC, KC), 1)).astype(BF16)

    def bias_body(cc, run):
        kk = key_s[chunk(cc), :]
        eq = kk == thr
        eqf = jnp.where(eq, 1.0, 0.0)
        rank = jnp.dot(tri, eqf.astype(BF16), preferred_element_type=F32) + run
        sel = (kk > thr) | (eq & (rank <= need))
        ok = sel & (kk > KEY_NEG_INF) & (kk < KEY_POS_INF)
        bias_s[chunk(cc), :] = jnp.where(ok, 0.0, NEG_INF)
        return run + jnp.sum(eqf, axis=0, keepdims=True)

    lax.fori_loop(0, nch, bias_body, jnp.zeros((1, Q_BLOCK), F32))

    q_t = (q_ref[0] * (HEAD_DIM ** -0.5)).T.astype(BF16)
    zeros_blk = jnp.zeros((HEAD_DIM, 2 * Q_BLOCK), BF16)
    for g in range(B_KV_HEADS):
        pair = jnp.concatenate([q_t[(2 * g) * HEAD_DIM:(2 * g + 1) * HEAD_DIM, :],
                                q_t[(2 * g + 1) * HEAD_DIM:(2 * g + 2) * HEAD_DIM, :]], axis=1)
        rhs = jnp.concatenate([pair if gg == g else zeros_blk for gg in range(B_KV_HEADS)], axis=0)

        def s_body(cc, mx):
            b = bias_s[chunk(cc), :]
            s = jnp.dot(kbf_s[chunk(cc), :], rhs, preferred_element_type=F32) + jnp.concatenate([b, b], axis=1)
            sc_s[chunk(cc), :] = s
            return jnp.maximum(mx, jnp.max(s.reshape(KC // 8, 8, 2 * Q_BLOCK), axis=0))

        mx = lax.fori_loop(0, nch, s_body, jnp.full((8, 2 * Q_BLOCK), NEG_INF, F32))
        mx = jnp.max(mx, axis=0, keepdims=True)

        def p_body(cc, carry):
            lsum, acc = carry
            pr = jnp.exp(sc_s[chunk(cc), :] - mx)
            lsum = lsum + jnp.sum(pr.reshape(KC // 8, 8, 2 * Q_BLOCK), axis=0)
            vt = vt_s[cc][g * HEAD_DIM:(g + 1) * HEAD_DIM, :]
            acc = acc + jnp.dot(vt, pr.astype(BF16), preferred_element_type=F32)
            return lsum, acc

        lsum, acc = lax.fori_loop(0, nch, p_body, (jnp.zeros((8, 2 * Q_BLOCK), F32),
                                                   jnp.zeros((HEAD_DIM, 2 * Q_BLOCK), F32)))
        o = acc / jnp.sum(lsum, axis=0, keepdims=True)
        ot_s[(2 * g) * HEAD_DIM:(2 * g + 1) * HEAD_DIM, :] = o[:, :Q_BLOCK]
        ot_s[(2 * g + 1) * HEAD_DIM:(2 * g + 2) * HEAD_DIM, :] = o[:, Q_BLOCK:]

    y_ref[0] = ot_s[...].T


def _dsa_prompt(q3, iq3, ikw3, k3, v3):
    bsz, t, _ = q3.shape
    nqb = t // Q_BLOCK
    blk = lambda b, j: (b, j, 0)
    full = lambda b, j: (b, 0, 0)
    kern = functools.partial(_dsa_prompt_kernel, T=t)
    return pl.pallas_call(
        kern,
        grid=(bsz, nqb),
        in_specs=[pl.BlockSpec((1, Q_BLOCK, B_WIDTH), blk),
                  pl.BlockSpec((1, Q_BLOCK, IQ_WIDTH), blk),
                  pl.BlockSpec((1, Q_BLOCK, LANES), blk),
                  pl.BlockSpec((1, t, LANES), full),
                  pl.BlockSpec((1, t, KV_WIDTH), full),
                  pl.BlockSpec((1, t, KV_WIDTH), full)],
        out_specs=pl.BlockSpec((1, Q_BLOCK, B_WIDTH), blk),
        out_shape=jax.ShapeDtypeStruct((bsz, t, B_WIDTH), F32),
        scratch_shapes=[pltpu.VMEM((t, KV_WIDTH), BF16),
                        pltpu.VMEM((t // KC, KV_WIDTH, KC), BF16),
                        pltpu.VMEM((t, IDX_DIM), BF16),
                        pltpu.VMEM((t, Q_BLOCK), I32),
                        pltpu.VMEM((t, Q_BLOCK), F32),
                        pltpu.VMEM((t, 2 * Q_BLOCK), F32),
                        pltpu.VMEM((B_WIDTH, Q_BLOCK), F32)],
        compiler_params=_cparams("parallel", "arbitrary"),
    )(q3, iq3, ikw3, ikw3, k3, v3)


PG = 8


def _sidx_kernel(pt_ref, iqf_ref, wcol_ref, *rest, Ts):
    pages = rest[:PG]
    out_ref = rest[PG]
    iqf = iqf_ref[0].astype(BF16)
    wcol = wcol_ref[0]
    for i in range(PG):
        s = _dot_nt(iqf, pages[i][0, 0])
        r = jnp.maximum(s, 0.0) * wcol
        out_ref[0, i] = jnp.sum(r.reshape(IDX_HEADS, Ts, PAGE), axis=0)


def _sidx(page_table, iqf, wcol, cache_idx_k, l, Ts):
    db, n_pages = page_table.shape
    steps = n_pages // PG

    def page_spec(i):
        return pl.BlockSpec((1, 1, PAGE, IDX_DIM), lambda b, j, pt: (l, pt[b, j * PG + i], 0, 0))

    grid_spec = pltpu.PrefetchScalarGridSpec(
        num_scalar_prefetch=1,
        grid=(db, steps),
        in_specs=[pl.BlockSpec((1, IDX_HEADS * Ts, IDX_DIM), lambda b, j, pt: (b, 0, 0)),
                  pl.BlockSpec((1, IDX_HEADS * Ts, 1), lambda b, j, pt: (b, 0, 0))]
                 + [page_spec(i) for i in range(PG)],
        out_specs=pl.BlockSpec((1, PG, Ts, PAGE), lambda b, j, pt: (b, j, 0, 0)),
    )
    return pl.pallas_call(
        functools.partial(_sidx_kernel, Ts=Ts),
        grid_spec=grid_spec,
        out_shape=jax.ShapeDtypeStruct((db, n_pages, Ts, PAGE), F32),
        compiler_params=_cparams("parallel", "arbitrary"),
    )(page_table, iqf, wcol, *([cache_idx_k] * PG))


def _ssel_kernel(sc_ref, iqf_ref, wcol_ref, iknew_ref, bias_ref, key_s, *, Ts, NT):
    key_s[0:NT] = _order_key(sc_ref[0])
    s = _dot_nt(iqf_ref[0], iknew_ref[0])
    r = jnp.maximum(s, 0.0) * wcol_ref[0]
    snew = jnp.sum(r.reshape(IDX_HEADS, Ts, PAGE), axis=0)
    vis = _iota((Ts, PAGE), 1) <= _iota((Ts, PAGE), 0)
    key_s[NT] = _order_key(jnp.where(vis, snew, NEG_INF))
    keys = key_s[...]

    def count(mask):
        part = jnp.sum(jnp.where(mask, 1.0, 0.0), axis=0)
        return jnp.sum(part, axis=1, keepdims=True)

    def bit_body(i, t):
        cand = t + jnp.left_shift(jnp.int32(1), 31 - i)
        cnt = count(keys >= cand[None])
        return jnp.where(cnt >= TOPK, cand, t)

    thr = lax.fori_loop(0, 32, bit_body, jnp.full((Ts, 1), INT_MIN, I32))
    need = TOPK - count(keys > thr[None])
    n_eq = count(keys == thr[None])
    finite = (keys > KEY_NEG_INF) & (keys < KEY_POS_INF)
    bias_ref[0] = jnp.where((keys >= thr[None]) & finite, 0.0, NEG_INF)
    has_tie = jnp.max(jnp.where((n_eq > need) & (thr > KEY_NEG_INF), 1.0, 0.0)) > 0.0

    @pl.when(has_tie)
    def _():
        triu = (_iota((PAGE, PAGE), 0) <= _iota((PAGE, PAGE), 1)).astype(BF16)

        def tile_body(j, run):
            kk = key_s[j]
            eq = kk == thr
            eqf = jnp.where(eq, 1.0, 0.0)
            rank = jnp.dot(eqf.astype(BF16), triu, preferred_element_type=F32) + run
            sel = (kk > thr) | (eq & (rank <= need))
            ok = sel & (kk > KEY_NEG_INF) & (kk < KEY_POS_INF)
            bias_ref[0, j] = jnp.where(ok, 0.0, NEG_INF)
            return run + jnp.sum(eqf, axis=1, keepdims=True)

        lax.fori_loop(0, NT + 1, tile_body, jnp.zeros((Ts, 1), F32))


def _ssel(scores, iqf, wcol, iknew):
    db, nt, ts, _ = scores.shape
    b3 = lambda b: (b, 0, 0)
    b4 = lambda b: (b, 0, 0, 0)
    return pl.pallas_call(
        functools.partial(_ssel_kernel, Ts=ts, NT=nt),
        grid=(db,),
        in_specs=[pl.BlockSpec((1, nt, ts, PAGE), b4),
                  pl.BlockSpec((1, IDX_HEADS * ts, IDX_DIM), b3),
                  pl.BlockSpec((1, IDX_HEADS * ts, 1), b3),
                  pl.BlockSpec((1, PAGE, IDX_DIM), b3)],
        out_specs=pl.BlockSpec((1, nt + 1, ts, PAGE), b4),
        out_shape=jax.ShapeDtypeStruct((db, nt + 1, ts, PAGE), F32),
        scratch_shapes=[pltpu.VMEM((nt + 1, ts, PAGE), I32)],
        compiler_params=_cparams("parallel"),
    )(scores, iqf, wcol, iknew)


def _sattn_kernel(pt_ref, qf_ref, bias_ref, biasn_ref, knew_ref, vnew_ref, *rest, Ts):
    kpages = rest[:PG]
    vpages = rest[PG:2 * PG]
    y_ref = rest[2 * PG]
    m_s, l_s, acc_s = rest[2 * PG + 1:]
    j = pl.program_id(1)
    R = 2 * Ts

    @pl.when(j == 0)
    def _():
        m_s[...] = jnp.full(m_s.shape, NEG_INF, F32)
        l_s[...] = jnp.zeros(l_s.shape, F32)
        acc_s[...] = jnp.zeros(acc_s.shape, F32)

    def update(g, s, pv):
        m_old = m_s[g]
        m_new = jnp.maximum(m_old, jnp.max(s, axis=1, keepdims=True))
        m_safe = jnp.where(m_new == NEG_INF, 0.0, m_new)
        scale = jnp.exp(m_old - m_safe)
        pr = jnp.exp(s - m_safe)
        l_s[g] = scale * l_s[g] + jnp.sum(pr, axis=1, keepdims=True)
        acc_s[g] = scale * acc_s[g] + pv(pr)
        m_s[g] = m_new

    for g in range(B_KV_HEADS):
        qg = qf_ref[0, g * R:(g + 1) * R, :].astype(BF16)
        s_parts = []
        for i in range(PG):
            kg = kpages[i][0, 0, :, g, :]
            b = bias_ref[0, i]
            s_parts.append(_dot_nt(qg, kg) + jnp.concatenate([b, b], axis=0))
        s = jnp.concatenate(s_parts, axis=1)

        def pv(pr, g=g):
            acc = jnp.zeros((R, HEAD_DIM), F32)
            for i in range(PG):
                acc = acc + _dot(pr[:, i * PAGE:(i + 1) * PAGE], vpages[i][0, 0, :, g, :])
            return acc

        update(g, s, pv)

    @pl.when(j == pl.num_programs(1) - 1)
    def _():
        bn = biasn_ref[0, 0]
        for g in range(B_KV_HEADS):
            qg = qf_ref[0, g * R:(g + 1) * R, :].astype(BF16)
            kn = knew_ref[0][:, g * HEAD_DIM:(g + 1) * HEAD_DIM]
            vn = vnew_ref[0][:, g * HEAD_DIM:(g + 1) * HEAD_DIM]
            s = _dot_nt(qg, kn) + jnp.concatenate([bn, bn], axis=0)
            update(g, s, lambda pr, vn=vn: _dot(pr, vn))
            o = acc_s[g] / l_s[g]
            y_ref[0, :, (2 * g) * HEAD_DIM:(2 * g + 1) * HEAD_DIM] = o[:Ts]
            y_ref[0, :, (2 * g + 1) * HEAD_DIM:(2 * g + 2) * HEAD_DIM] = o[Ts:]


def _sattn(page_table, qf, bias, knew, vnew, cache_k, cache_v, l, Ts):
    db, n_pages = page_table.shape
    steps = n_pages // PG

    def page_spec(i):
        return pl.BlockSpec((1, 1, PAGE, B_KV_HEADS, HEAD_DIM),
                            lambda b, j, pt: (l, pt[b, j * PG + i], 0, 0, 0))

    b3 = lambda b, j, pt: (b, 0, 0)
    grid_spec = pltpu.PrefetchScalarGridSpec(
        num_scalar_prefetch=1,
        grid=(db, steps),
        in_specs=[pl.BlockSpec((1, B_HEADS * Ts, HEAD_DIM), b3),
                  pl.BlockSpec((1, PG, Ts, PAGE), lambda b, j, pt: (b, j, 0, 0)),
                  pl.BlockSpec((1, 1, Ts, PAGE), lambda b, j, pt: (b, n_pages, 0, 0)),
                  pl.BlockSpec((1, PAGE, KV_WIDTH), b3),
                  pl.BlockSpec((1, PAGE, KV_WIDTH), b3)]
                 + [page_spec(i) for i in range(PG)] + [page_spec(i) for i in range(PG)],
        out_specs=pl.BlockSpec((1, Ts, B_WIDTH), b3),
        scratch_shapes=[pltpu.VMEM((B_KV_HEADS, 2 * Ts, 1), F32),
                        pltpu.VMEM((B_KV_HEADS, 2 * Ts, 1), F32),
                        pltpu.VMEM((B_KV_HEADS, 2 * Ts, HEAD_DIM), F32)],
    )
    return pl.pallas_call(
        functools.partial(_sattn_kernel, Ts=Ts),
        grid_spec=grid_spec,
        out_shape=jax.ShapeDtypeStruct((db, Ts, B_WIDTH), F32),
        compiler_params=_cparams("parallel", "arbitrary"),
    )(page_table, qf, bias, bias, knew, vnew, *([cache_k] * PG), *([cache_v] * PG))


def _dsa_sample(q3, iq3, ikw3, k3, v3, cache_k, cache_v, cache_idx_k, page_table, l):
    db, ts, _ = q3.shape
    iqf = iq3.reshape(db, ts, IDX_HEADS, IDX_DIM).transpose(0, 2, 1, 3).reshape(db, IDX_HEADS * ts, IDX_DIM)
    iw = ikw3[:, :, IDX_DIM:IDX_DIM + IDX_HEADS] * (IDX_DIM ** -0.5 * IDX_HEADS ** -0.5)
    wcol = iw.transpose(0, 2, 1).reshape(db, IDX_HEADS * ts, 1)
    qf = (q3 * (HEAD_DIM ** -0.5)).reshape(db, ts, B_HEADS, HEAD_DIM).transpose(0, 2, 1, 3)
    qf = qf.reshape(db, B_HEADS * ts, HEAD_DIM)
    pad = lambda a: jnp.pad(a, ((0, 0), (0, PAGE - ts), (0, 0)))
    iknew = pad(ikw3[:, :, :IDX_DIM])
    scores = _sidx(page_table, iqf, wcol, cache_idx_k, l, ts)
    bias = _ssel(scores, iqf, wcol, iknew)
    return _sattn(page_table, qf, bias, pad(k3), pad(v3), cache_k, cache_v, l, ts)


def _ffn_kernel(x_ref, ya_ref, yb_ref, yc_ref, wo_ref, gf_ref, wg_ref, wu_ref, wd_ref, gfin_ref,
                o_ref, *, final):
    mixed = (jnp.dot(ya_ref[...].astype(BF16), wo_ref[0:A_WIDTH, :], preferred_element_type=F32)
             + jnp.dot(yb_ref[...].astype(BF16), wo_ref[A_WIDTH:A_WIDTH + B_WIDTH, :],
                       preferred_element_type=F32)
             + jnp.dot(yc_ref[...].astype(BF16), wo_ref[A_WIDTH + B_WIDTH:, :],
                       preferred_element_type=F32))
    x1 = x_ref[...] + mixed
    ms = jnp.mean(x1 * x1, axis=-1, keepdims=True)
    hf = (x1 * lax.rsqrt(ms + NORM_EPS) * gf_ref[...]).astype(BF16)
    gate = jnp.dot(hf, wg_ref[...], preferred_element_type=F32)
    up = jnp.dot(hf, wu_ref[...], preferred_element_type=F32)
    act = (gate * _sigmoid(gate) * up).astype(BF16)
    x2 = x1 + jnp.dot(act, wd_ref[...], preferred_element_type=F32)
    if final:
        ms2 = jnp.mean(x2 * x2, axis=-1, keepdims=True)
        x2 = x2 * lax.rsqrt(ms2 + NORM_EPS) * gfin_ref[...]
    o_ref[...] = x2


def _ffn(x2d, ya, yb, yc, wo, gf, wg, wu, wd, gfin, final, tm):
    n, d = x2d.shape
    dff = wg.shape[1]
    row = lambda i: (i, 0)
    fixed = lambda i: (0, 0)
    once = dict(pipeline_mode=pl.Buffered(1))
    return pl.pallas_call(
        functools.partial(_ffn_kernel, final=final),
        grid=(n // tm,),
        in_specs=[pl.BlockSpec((tm, d), row), pl.BlockSpec((tm, A_WIDTH), row),
                  pl.BlockSpec((tm, B_WIDTH), row), pl.BlockSpec((tm, C_WIDTH), row),
                  pl.BlockSpec((d, d), fixed, **once), pl.BlockSpec((1, d), fixed),
                  pl.BlockSpec((d, dff), fixed, **once), pl.BlockSpec((d, dff), fixed, **once),
                  pl.BlockSpec((dff, d), fixed, **once), pl.BlockSpec((1, d), fixed)],
        out_specs=pl.BlockSpec((tm, d), row),
        out_shape=jax.ShapeDtypeStruct((n, d), F32),
        compiler_params=_cparams("parallel"),
    )(x2d, ya, yb, yc, wo, gf, wg, wu, wd, gfin)


def _rope_tables(pos):
    half = ROT_DIM // 2
    inv = ROPE_THETA ** (-jnp.arange(half, dtype=F32) / half)
    ang = pos.astype(F32)[:, None] * inv[None, :]
    j = jnp.arange(LANES) % HEAD_DIM
    cos = jnp.cos(ang)[:, j % half]
    sin = jnp.sin(ang)[:, j % half]
    cos_t = jnp.where(j[None, :] < ROT_DIM, cos, 1.0)
    sina = jnp.where(j[None, :] < half, -sin, 0.0)
    sinb = jnp.where((j[None, :] >= half) & (j[None, :] < ROT_DIM), sin, 0.0)
    return cos_t, sina, sinb


def _pad_w_in(w):
    d = w.shape[0]
    z = lambda n: jnp.zeros((d, n), w.dtype)
    return jnp.concatenate([w[:, :A_PROJ], z(A_PAD - A_PROJ),
                            w[:, A_PROJ:A_PROJ + B_PROJ], z(B_PAD - B_PROJ),
                            w[:, A_PROJ + B_PROJ:]], axis=1).astype(BF16)


def _layer(x3, l, tabs, mstate, shift0, s0, attn_fn, p, wts, final, tm):
    bsz, t, d = x3.shape
    n = bsz * t
    pa, q, k, v, iq, ikw, pc = _inproj(x3.reshape(n, d), p['norm_mix'][l].reshape(1, d), wts['w_in'][l], tabs, tm)
    r3 = lambda a: a.reshape(bsz, t, a.shape[-1])
    y_a, c_new, n_new, m_new = _mlstm(r3(pa), *mstate, p['mlstm_gate_b'][l], p['mlstm_norm'][l])
    y_b = attn_fn(r3(q), r3(iq), r3(ikw), r3(k), r3(v))
    y_c, shift, s_new = _rwkv(r3(pc), shift0, s0, p, l)
    x_new = _ffn(x3.reshape(n, d), y_a.reshape(n, -1), y_b.reshape(n, -1), y_c.reshape(n, -1),
                 wts['w_out'][l], p['norm_ffn'][l].reshape(1, d), wts['w_gate'][l], wts['w_up'][l],
                 wts['w_down'][l], p['norm_final'].reshape(1, d), final, tm)
    new = (k.reshape(bsz, t, B_KV_HEADS, HEAD_DIM), v.reshape(bsz, t, B_KV_HEADS, HEAD_DIM),
           r3(ikw)[:, :, :IDX_DIM], c_new, n_new, m_new, shift, s_new)
    return x_new.reshape(bsz, t, d), new


def kernel(x_prompt, x_sample, cache_k, cache_v, cache_idx_k, state_mlstm_C, state_mlstm_n, state_mlstm_m, state_rwkv_shift, state_rwkv_S, page_table, norm_mix, w_in, mlstm_gate_b, mlstm_norm, rwkv_mu, rwkv_w0, rwkv_w2, rwkv_a0, rwkv_a2, rwkv_g2, rwkv_k_k, rwkv_k_a, rwkv_r_k, rwkv_ln_w, rwkv_ln_b, w_out, norm_ffn, w_gate, w_up, w_down, norm_final):
    p = dict(norm_mix=norm_mix, mlstm_gate_b=mlstm_gate_b, mlstm_norm=mlstm_norm,
             rwkv_mu=rwkv_mu, rwkv_w0=rwkv_w0, rwkv_w2=rwkv_w2, rwkv_a0=rwkv_a0, rwkv_a2=rwkv_a2,
             rwkv_g2=rwkv_g2, rwkv_k_k=rwkv_k_k, rwkv_k_a=rwkv_k_a, rwkv_r_k=rwkv_r_k,
             rwkv_ln_w=rwkv_ln_w, rwkv_ln_b=rwkv_ln_b, norm_ffn=norm_ffn, norm_final=norm_final)
    depth = w_in.shape[0]
    wts = dict(w_in=[_pad_w_in(w_in[l]) for l in range(depth)],
               w_out=w_out.astype(BF16), w_gate=w_gate.astype(BF16),
               w_up=w_up.astype(BF16), w_down=w_down.astype(BF16))
    bp, tp, _ = x_prompt.shape
    db, ts, _ = x_sample.shape
    past = page_table.shape[1] * PAGE
    tabs_p = _rope_tables(jnp.arange(tp, dtype=I32))
    tabs_s = tuple(jnp.tile(a, (db, 1)) for a in _rope_tables(past + jnp.arange(ts, dtype=I32)))
    tm_p = min(256, bp * tp)
    tm_s = db * ts
    zc = jnp.zeros((bp, A_HEADS, HEAD_DIM, HEAD_DIM), F32)
    zn = jnp.zeros((bp, A_HEADS, HEAD_DIM), F32)
    zm = jnp.zeros((bp, A_HEADS), F32)
    zshift = jnp.zeros((bp, C_PROJ), F32)
    zs = jnp.zeros((bp, C_HEADS, HEAD_DIM, HEAD_DIM), F32)
    xp, xs = x_prompt, x_sample
    new_p, new_s = [], []
    for l in range(depth):
        final = l == depth - 1
        xp, st = _layer(xp, l, tabs_p, (zc, zn, zm), zshift, zs, _dsa_prompt, p, wts, final, tm_p)
        new_p.append(st)
        attn_s = functools.partial(_dsa_sample, cache_k=cache_k, cache_v=cache_v, cache_idx_k=cache_idx_k,
                                   page_table=page_table, l=l)
        xs, st = _layer(xs, l, tabs_s, (state_mlstm_C[l], state_mlstm_n[l], state_mlstm_m[l]),
                        state_rwkv_shift[l], state_rwkv_S[l], attn_s, p, wts, final, tm_s)
        new_s.append(st)
    stack = lambda states, i: jnp.stack([st[i] for st in states])
    outs_p = [stack(new_p, i) for i in range(8)]
    outs_s = [stack(new_s, i) for i in range(8)]
    return (xp, xs, *outs_p, *outs_s)
```

```python
import functools
import math

import jax
import jax.numpy as jnp
from jax import lax
from jax.experimental import pallas as pl
from jax.experimental.pallas import tpu as pltpu

F32 = jnp.float32
BF16 = jnp.bfloat16
I32 = jnp.int32

HEAD_DIM = 64
A_HEADS = 4
B_HEADS = 8
B_KV_HEADS = 4
C_HEADS = 4
IDX_HEADS = 8
IDX_DIM = 64
TOPK = 256
Q_BLOCK = 128
ROT_DIM = 16
ROPE_THETA = 500000.0
MLSTM_CHUNK = 64
RWKV_CHUNK = 64
SEQ_ROWS_PER_STEP = 4
RWKV_LN_EPS = 64e-5
NORM_EPS = 1e-6
PAGE = 128

A_WIDTH = A_HEADS * HEAD_DIM
B_WIDTH = B_HEADS * HEAD_DIM
KV_WIDTH = B_KV_HEADS * HEAD_DIM
C_WIDTH = C_HEADS * HEAD_DIM
IQ_WIDTH = IDX_HEADS * IDX_DIM
A_PROJ = 4 * A_WIDTH + 2 * A_HEADS
B_PROJ = B_WIDTH + 2 * KV_WIDTH + IQ_WIDTH + IDX_DIM + IDX_HEADS
C_PROJ = 3 * C_WIDTH + 64 + 64 + 128

LANES = 128
A_PAD = 4 * A_WIDTH + LANES
B_PAD = B_PROJ + (LANES - (IDX_DIM + IDX_HEADS))
P_PAD = A_PAD + B_PAD + C_PROJ

VMEM_LIMIT = 56 * 1024 * 1024

INT_MIN = -2147483648
KEY_NEG_INF = -2139095040
KEY_POS_INF = 2139095040
NEG_INF = float("-inf")


def _cparams(*sem):
    return pltpu.CompilerParams(dimension_semantics=sem, vmem_limit_bytes=VMEM_LIMIT)


def _dot(a, b):
    return jnp.dot(a.astype(BF16), b.astype(BF16), preferred_element_type=F32)


def _dot_nt(a, b):
    return lax.dot_general(a.astype(BF16), b.astype(BF16), (((1,), (1,)), ((), ())),
                           preferred_element_type=F32)


def _dot_tn(a, b):
    return lax.dot_general(a.astype(BF16), b.astype(BF16), (((0,), (0,)), ((), ())),
                           preferred_element_type=F32)


def _split2(a):
    hi = a.astype(BF16)
    lo = (a - hi.astype(F32)).astype(BF16)
    return hi, lo


def _split3(a):
    hi = a.astype(BF16)
    r = a - hi.astype(F32)
    mid = r.astype(BF16)
    lo = (r - mid.astype(F32)).astype(BF16)
    return hi, mid, lo


_NN = (((1,), (0,)), ((), ()))
_NT = (((1,), (1,)), ((), ()))
_TN = (((0,), (0,)), ((), ()))


def _dg(a, b, dims):
    return lax.dot_general(a, b, dims, preferred_element_type=F32)


def _dot3(a, b, dims=_NN):
    ah, al = _split2(a)
    bh, bl = _split2(b)
    return _dg(ah, bh, dims) + (_dg(ah, bl, dims) + _dg(al, bh, dims))


def _dot01_left(m01, x):
    m = m01.astype(BF16)
    x1, x2, x3 = _split3(x)
    return _dg(m, x1, _NN) + (_dg(m, x2, _NN) + _dg(m, x3, _NN))


def _dot01_right(x, m01):
    m = m01.astype(BF16)
    x1, x2, x3 = _split3(x)
    return _dg(x1, m, _NN) + (_dg(x2, m, _NN) + _dg(x3, m, _NN))


def _sigmoid(x):
    return 1.0 / (1.0 + jnp.exp(-x))


def _softplus(x):
    return jnp.maximum(x, 0.0) + jnp.log1p(jnp.exp(-jnp.abs(x)))


def _iota(shape, dim):
    return lax.broadcasted_iota(I32, shape, dim)


def _rope_tile(xt, cos, sina, sinb):
    up = pltpu.roll(xt, LANES - ROT_DIM // 2, axis=1)
    dn = pltpu.roll(xt, ROT_DIM // 2, axis=1)
    return xt * cos + up * sina + dn * sinb


def _inproj_kernel(x_ref, g_ref, w_ref, cos_ref, sina_ref, sinb_ref,
                   pa_ref, q_ref, k_ref, v_ref, iq_ref, ikw_ref, pc_ref):
    x = x_ref[...]
    ms = jnp.mean(x * x, axis=-1, keepdims=True)
    h = (x * lax.rsqrt(ms + NORM_EPS) * g_ref[...]).astype(BF16)
    cos = cos_ref[...]
    sina = sina_ref[...]
    sinb = sinb_ref[...]

    def proj(lo, width):
        return jnp.dot(h, w_ref[:, lo:lo + width], preferred_element_type=F32)

    def rope(x2):
        tiles = [_rope_tile(x2[:, j * LANES:(j + 1) * LANES], cos, sina, sinb)
                 for j in range(x2.shape[1] // LANES)]
        return tiles[0] if len(tiles) == 1 else jnp.concatenate(tiles, axis=1)

    pa_ref[...] = proj(0, A_PAD)
    b0 = A_PAD
    q_ref[...] = rope(proj(b0, B_WIDTH))
    k_ref[...] = rope(proj(b0 + B_WIDTH, KV_WIDTH))
    v_ref[...] = proj(b0 + B_WIDTH + KV_WIDTH, KV_WIDTH)
    iq_ref[...] = rope(proj(b0 + B_WIDTH + 2 * KV_WIDTH, IQ_WIDTH))
    ikw = proj(b0 + B_WIDTH + 2 * KV_WIDTH + IQ_WIDTH, LANES)
    lane = _iota(ikw.shape, 1)
    ikw_ref[...] = jnp.where(lane < IDX_DIM, _rope_tile(ikw, cos, sina, sinb), ikw)
    pc_ref[...] = proj(A_PAD + B_PAD, C_PROJ)


def _inproj(x2d, g, w_bf, tabs, tm):
    n, d = x2d.shape
    cos, sina, sinb = tabs
    tab_blocks = cos.shape[0] // tm
    row = lambda i: (i, 0)
    fixed = lambda i: (0, 0)
    tab = lambda i: (i % tab_blocks, 0)
    widths = (A_PAD, B_WIDTH, KV_WIDTH, KV_WIDTH, IQ_WIDTH, LANES, C_PROJ)
    return pl.pallas_call(
        _inproj_kernel,
        grid=(n // tm,),
        in_specs=[pl.BlockSpec((tm, d), row), pl.BlockSpec((1, d), fixed),
                  pl.BlockSpec((d, P_PAD), fixed),
                  pl.BlockSpec((tm, LANES), tab), pl.BlockSpec((tm, LANES), tab),
                  pl.BlockSpec((tm, LANES), tab)],
        out_specs=[pl.BlockSpec((tm, w), row) for w in widths],
        out_shape=[jax.ShapeDtypeStruct((n, w), F32) for w in widths],
        compiler_params=_cparams("parallel"),
    )(x2d, g, w_bf, cos, sina, sinb)


def _mlstm_kernel(pa_ref, c0_ref, n0_ref, m0_ref, gb_ref, gn_ref,
                  y_ref, cout_ref, nout_ref, mout_ref, c_s, n_s, m_s, *, L, Lp, NB):
    c = pl.program_id(1)

    @pl.when(c == 0)
    def _():
        c_s[...] = c0_ref[...]
        n_s[...] = n0_ref[...]
        m_s[...] = m0_ref[...]

    row = _iota((Lp, Lp), 0)
    col = _iota((Lp, Lp), 1)
    eye = row == col
    causal = row >= col
    valid = _iota((Lp, LANES), 0) < L

    def to_row(colvec):
        return jnp.sum(jnp.where(eye, colvec, 0.0), axis=0, keepdims=True)

    for bb in range(NB):
        pa = pa_ref[bb]
        if Lp > L:
            pa = jnp.concatenate([pa, jnp.zeros((Lp - L, pa.shape[1]), F32)], axis=0)
        gates = pa[:, 4 * A_WIDTH:4 * A_WIDTH + LANES] + gb_ref[...]
        log_i = jnp.where(valid, gates, NEG_INF)
        log_f = jnp.where(valid, -_softplus(-gates), 0.0)
        bcum = _dot01_left(causal, log_f)
        for h in range(A_HEADS):
            sl = slice(h * HEAD_DIM, (h + 1) * HEAD_DIM)
            b_col = bcum[:, A_HEADS + h:A_HEADS + h + 1]
            i_col = log_i[:, h:h + 1]
            b_row = to_row(b_col)
            i_row = to_row(i_col)
            m_prev = m_s[bb, 0:1, h:h + 1]
            dmat = jnp.where(causal, b_col - b_row + i_row, NEG_INF)
            inter = b_col + m_prev
            m_t = jnp.maximum(inter, jnp.max(dmat, axis=1, keepdims=True))
            qh = pa[:, sl]
            kh = pa[:, A_WIDTH + h * HEAD_DIM:A_WIDTH + (h + 1) * HEAD_DIM] * (HEAD_DIM ** -0.5)
            vh = pa[:, 2 * A_WIDTH + h * HEAD_DIM:2 * A_WIDTH + (h + 1) * HEAD_DIM]
            oh = pa[:, 3 * A_WIDTH + h * HEAD_DIM:3 * A_WIDTH + (h + 1) * HEAD_DIM]
            s = _dot_nt(qh, kh) * jnp.exp(dmat - m_t)
            w_inter = jnp.exp(inter - m_t)
            c_h = c_s[bb, h]
            n_row = n_s[bb, h:h + 1, :]
            num = _dot(s, vh) + w_inter * _dot_nt(qh, c_h)
            den = jnp.sum(s, axis=1, keepdims=True) + w_inter * jnp.sum(qh * n_row, axis=1, keepdims=True)
            hh = num / jnp.maximum(jnp.abs(den), jnp.exp(-m_t))
            b_last = bcum[Lp - 1:Lp, A_HEADS + h:A_HEADS + h + 1]
            g_row = b_last - b_row + i_row
            g_col = b_last - b_col + i_col
            m_new = jnp.maximum(b_last + m_prev, jnp.max(g_row, axis=1, keepdims=True))
            wk_col = jnp.exp(g_col - m_new)
            dec = jnp.exp(b_last + m_prev - m_new)
            c_s[bb, h] = dec * c_h + _dot_tn(vh * wk_col, kh)
            n_s[bb, h:h + 1, :] = dec * n_row + jnp.sum(kh * wk_col, axis=0, keepdims=True)
            m_s[bb, 0:1, h:h + 1] = m_new
            ms = jnp.mean(hh * hh, axis=1, keepdims=True)
            hn = hh * lax.rsqrt(ms + NORM_EPS) * gn_ref[:, sl]
            y_ref[bb, :, sl] = (_sigmoid(oh) * hn)[:L]

    @pl.when(c == pl.num_programs(1) - 1)
    def _():
        cout_ref[...] = c_s[...]
        nout_ref[...] = n_s[...]
        mout_ref[...] = m_s[...]


def _mlstm(pa3, c0, n0, m0, gate_b, gnorm):
    bsz, t, _ = pa3.shape
    L = min(t, MLSTM_CHUNK)
    Lp = MLSTM_CHUNK
    nc = t // L
    NB = math.gcd(bsz, SEQ_ROWS_PER_STEP)
    gb = jnp.zeros((1, LANES), F32).at[0, :2 * A_HEADS].set(gate_b)
    kern = functools.partial(_mlstm_kernel, L=L, Lp=Lp, NB=NB)
    st4 = lambda b, c: (b, 0, 0, 0)
    st3 = lambda b, c: (b, 0, 0)
    fixed = lambda b, c: (0, 0)
    y, cn, nn, mn = pl.pallas_call(
        kern,
        grid=(bsz // NB, nc),
        in_specs=[pl.BlockSpec((NB, L, A_PAD), lambda b, c: (b, c, 0)),
                  pl.BlockSpec((NB, A_HEADS, HEAD_DIM, HEAD_DIM), st4),
                  pl.BlockSpec((NB, A_HEADS, HEAD_DIM), st3),
                  pl.BlockSpec((NB, 1, A_HEADS), st3),
                  pl.BlockSpec((1, LANES), fixed),
                  pl.BlockSpec((1, A_WIDTH), fixed)],
        out_specs=[pl.BlockSpec((NB, L, A_WIDTH), lambda b, c: (b, c, 0)),
                   pl.BlockSpec((NB, A_HEADS, HEAD_DIM, HEAD_DIM), st4),
                   pl.BlockSpec((NB, A_HEADS, HEAD_DIM), st3),
                   pl.BlockSpec((NB, 1, A_HEADS), st3)],
        out_shape=[jax.ShapeDtypeStruct((bsz, t, A_WIDTH), F32),
                   jax.ShapeDtypeStruct((bsz, A_HEADS, HEAD_DIM, HEAD_DIM), F32),
                   jax.ShapeDtypeStruct((bsz, A_HEADS, HEAD_DIM), F32),
                   jax.ShapeDtypeStruct((bsz, 1, A_HEADS), F32)],
        scratch_shapes=[pltpu.VMEM((NB, A_HEADS, HEAD_DIM, HEAD_DIM), F32),
                        pltpu.VMEM((NB, A_HEADS, HEAD_DIM), F32),
                        pltpu.VMEM((NB, 1, A_HEADS), F32)],
        compiler_params=_cparams("parallel", "arbitrary"),
    )(pa3, c0, n0, m0.reshape(bsz, 1, A_HEADS), gb, gnorm.reshape(1, A_WIDTH))
    return y, cn, nn, mn.reshape(bsz, A_HEADS)


def _rwkv_kernel(pc_ref, shift0_ref, s0_ref, mu_ref, w0_ref, w2_ref, a0_ref, a2_ref, g2_ref,
                 kk_ref, ka_ref, rk_ref, lnw_ref, lnb_ref,
                 y_ref, shift_out_ref, s_out_ref, s_s, carry_s, *, L, C, NB):
    c = pl.program_id(1)
    last = c == pl.num_programs(1) - 1
    W = C_WIDTH
    HC = C_HEADS * C

    @pl.when(c == 0)
    def _():
        carry_s[...] = shift0_ref[...]
        s_s[...] = jnp.zeros((NB, W, W), F32)
        for bb in range(NB):
            for h in range(C_HEADS):
                sl = slice(h * HEAD_DIM, (h + 1) * HEAD_DIM)
                s_s[bb, sl, sl] = s0_ref[bb, h]

    rowc = _iota((C, C_PROJ), 0)
    lane_head = _iota((W, W), 0) // HEAD_DIM == _iota((W, W), 1) // HEAD_DIM
    tri = _iota((C, C), 0) >= _iota((C, C), 1)
    lane_h = _iota((C, W), 1) // HEAD_DIM
    vrow = _iota((C, W), 0) < L
    tt = _iota((HC, HC), 0) % C
    ii = _iota((HC, HC), 1) % C
    eye = (_iota((HC, HC), 0) == _iota((HC, HC), 1)).astype(F32)

    def head_sum(x):
        return _dot01_right(x, lane_head)

    def stack(x):
        return jnp.concatenate([jnp.where(lane_h == h, x, 0.0) for h in range(C_HEADS)], axis=0)

    def level_mask(s):
        return ((tt // s) % 2 == 1) & ((ii // s) % 2 == 0) & (tt // (2 * s) == ii // (2 * s))

    for bb in range(NB):
        pc = pc_ref[bb]
        if C > L:
            pc = jnp.concatenate([pc, jnp.zeros((C - L, pc.shape[1]), F32)], axis=0)
        prev = jnp.where(rowc == 0, carry_s[bb], pltpu.roll(pc, 1, axis=0))
        carry_s[bb] = pc[L - 1:L, :]
        pcs = pc + mu_ref[...] * (prev - pc)
        cr = pcs[:, 0:W]
        ck = pcs[:, W:2 * W]
        cv = pcs[:, 2 * W:3 * W]
        cwl = pcs[:, 3 * W:3 * W + 64]
        cal = pcs[:, 3 * W + 64:3 * W + 128]
        cgl = pcs[:, 3 * W + 128:3 * W + 256]
        wlog = -_softplus(-(w0_ref[...] + _dot(jnp.tanh(cwl), w2_ref[...]))) - 0.5
        lw = -jnp.exp(wlog)
        a = _sigmoid(a0_ref[...] + _dot(cal, a2_ref[...]))
        g = _dot(_sigmoid(cgl), g2_ref[...])
        kk = ck * kk_ref[...]
        kkn = kk / jnp.maximum(jnp.sqrt(head_sum(kk * kk)), 1e-12)
        k2 = ck * (1.0 + (a - 1.0) * ka_ref[...])
        alpha = -kkn
        beta = kkn * a
        if C > L:
            zero = lambda z: jnp.where(vrow, z, 0.0)
            lw, alpha, beta, k2s, cvs, crs = zero(lw), zero(alpha), zero(beta), zero(k2), zero(cv), zero(cr)
        else:
            k2s, cvs, crs = k2, cv, cr

        logp = _dot01_left(tri, lw)
        p = jnp.exp(logp)
        pinv = jnp.exp(-logp)
        pprev = jnp.exp(logp - lw)
        p_last = p[C - 1:C, :]
        a_st, r_st = stack(alpha * pprev), stack(crs * p)
        b_st, k_st, v_st = stack(beta * pinv), stack(k2s * pinv), stack(cvs)
        a_bf, b_bf, k_bf, v_bf = a_st.astype(BF16), b_st.astype(BF16), k_st.astype(BF16), v_st.astype(BF16)
        ar = jnp.concatenate([a_bf, r_st.astype(BF16)], axis=0)
        gb = _dg(ar, b_bf, _NT)
        gk = _dg(ar, k_bf, _NT)
        l_ab = jnp.where(tt > ii, gb[:HC], 0.0)
        l_rb = jnp.where(tt >= ii, gb[HC:], 0.0).astype(BF16)
        l_ak = jnp.where(tt > ii, gk[:HC], 0.0).astype(BF16)
        l_rk = jnp.where(tt >= ii, gk[HC:], 0.0).astype(BF16)

        x = eye + jnp.where(level_mask(1), l_ab, 0.0)
        s = 2
        while s < C:
            e = jnp.where(level_mask(s), l_ab, 0.0).astype(BF16)
            xb = x.astype(BF16)
            x = x + _dg(_dg(xb, e, _NN).astype(BF16), xb, _NN)
            s *= 2

        xb = x.astype(BF16)
        a_t = _dg(xb, a_bf, _NN)
        v_t = _dg(xb, _dg(l_ak, v_bf, _NN).astype(BF16), _NN)
        r_q = r_st + _dg(l_rb, a_t.astype(BF16), _NN)
        y0 = _dg(l_rb, v_t.astype(BF16), _NN) + _dg(l_rk, v_bf, _NN)
        s_old = s_s[bb]
        u = _dot3(a_t, s_old, _NT) + v_t
        y_st = _dg(r_q.astype(BF16), s_old.astype(BF16), _NT) + y0
        s_new = (s_old + _dot3(u, b_st, _TN) + _dot3(v_st, k_st, _TN)) * p_last
        s_s[bb] = s_new
        yc = y_st[0:C]
        for h in range(1, C_HEADS):
            yc = yc + y_st[h * C:(h + 1) * C]

        inv_d = 1.0 / HEAD_DIM
        mean = head_sum(yc) * inv_d
        dlt = yc - mean
        var = head_sum(dlt * dlt) * inv_d
        ycn = dlt * lax.rsqrt(var + RWKV_LN_EPS) * lnw_ref[...] + lnb_ref[...]
        bonus = head_sum(cr * k2 * rk_ref[...]) * cv
        y_ref[bb] = ((ycn + bonus) * g)[:L]

    @pl.when(last)
    def _():
        shift_out_ref[...] = carry_s[...]
        for bb in range(NB):
            for h in range(C_HEADS):
                sl = slice(h * HEAD_DIM, (h + 1) * HEAD_DIM)
                s_out_ref[bb, h] = s_s[bb, sl, sl]


def _rwkv(pc3, shift0, s0, p, l):
    bsz, t, _ = pc3.shape
    C = RWKV_CHUNK
    L = min(t, C)
    nc = t // L
    NB = math.gcd(bsz, SEQ_ROWS_PER_STEP)
    r1 = lambda a: a.reshape(1, -1)
    params = [r1(p['rwkv_mu'][l]), r1(p['rwkv_w0'][l]), p['rwkv_w2'][l], r1(p['rwkv_a0'][l]),
              p['rwkv_a2'][l], p['rwkv_g2'][l], r1(p['rwkv_k_k'][l]), r1(p['rwkv_k_a'][l]),
              r1(p['rwkv_r_k'][l]), r1(p['rwkv_ln_w'][l]), r1(p['rwkv_ln_b'][l])]
    fixed = lambda b, c: (0, 0)
    pspecs = [pl.BlockSpec(a.shape, fixed) for a in params]
    kern = functools.partial(_rwkv_kernel, L=L, C=C, NB=NB)
    y, shift, s_new = pl.pallas_call(
        kern,
        grid=(bsz // NB, nc),
        in_specs=[pl.BlockSpec((NB, L, C_PROJ), lambda b, c: (b, c, 0)),
                  pl.BlockSpec((NB, 1, C_PROJ), lambda b, c: (b, 0, 0)),
                  pl.BlockSpec((NB, C_HEADS, HEAD_DIM, HEAD_DIM), lambda b, c: (b, 0, 0, 0))] + pspecs,
        out_specs=[pl.BlockSpec((NB, L, C_WIDTH), lambda b, c: (b, c, 0)),
                   pl.BlockSpec((NB, 1, C_PROJ), lambda b, c: (b, 0, 0)),
                   pl.BlockSpec((NB, C_HEADS, HEAD_DIM, HEAD_DIM), lambda b, c: (b, 0, 0, 0))],
        out_shape=[jax.ShapeDtypeStruct((bsz, t, C_WIDTH), F32),
                   jax.ShapeDtypeStruct((bsz, 1, C_PROJ), F32),
                   jax.ShapeDtypeStruct((bsz, C_HEADS, HEAD_DIM, HEAD_DIM), F32)],
        scratch_shapes=[pltpu.VMEM((NB, C_WIDTH, C_WIDTH), F32), pltpu.VMEM((NB, 1, C_PROJ), F32)],
        compiler_params=_cparams("parallel", "arbitrary"),
    )(pc3, shift0.reshape(bsz, 1, C_PROJ), s0, *params)
    return y, shift.reshape(bsz, C_PROJ), s_new


def _order_key(score):
    bits = pltpu.bitcast(score, I32)
    return jnp.where(bits < 0, (bits ^ 0x7FFFFFFF) + 1, bits)


KC = 256
KEY_CLASSES = 4


def _dsa_prompt_block(nk, qb, iq_t, w_t, q_t, kbf_s, vt_s, ikbf_s, key_s, bias_s, ot_s):
    ik = ikbf_s[0:nk, :]
    acc = jnp.zeros((nk, Q_BLOCK), F32)
    for h in range(IDX_HEADS):
        st = jnp.dot(ik, iq_t[h * IDX_DIM:(h + 1) * IDX_DIM, :], preferred_element_type=F32)
        acc = acc + jnp.maximum(st, 0.0) * w_t[h:h + 1, :]
    visible = _iota((nk, Q_BLOCK), 0) <= qb * Q_BLOCK + _iota((nk, Q_BLOCK), 1)
    key_s[0:nk, :] = _order_key(jnp.where(visible, acc, NEG_INF))

    def count(pred):
        return jnp.sum(jnp.where(pred(key_s[0:nk, :]), 1.0, 0.0), axis=0, keepdims=True)

    def bit_body(i, t):
        cand = t + jnp.left_shift(jnp.int32(1), 31 - i)
        return jnp.where(count(lambda kk: kk >= cand) >= TOPK, cand, t)

    thr = lax.fori_loop(0, 32, bit_body, jnp.full((1, Q_BLOCK), INT_MIN, I32))
    need = TOPK - count(lambda kk: kk > thr)
    n_eq = count(lambda kk: kk == thr)
    keys = key_s[0:nk, :]
    floor = jnp.maximum(thr, KEY_NEG_INF + 1)
    bias_s[0:nk, :] = jnp.where((keys >= floor) & (keys < KEY_POS_INF), 0.0, NEG_INF)
    has_tie = jnp.max(jnp.where((n_eq > need) & (thr > KEY_NEG_INF), 1.0, 0.0)) > 0.0

    @pl.when(has_tie)
    def _():
        tri = (_iota((KC, KC), 0) >= _iota((KC, KC), 1)).astype(BF16)
        run = jnp.zeros((1, Q_BLOCK), F32)
        for cc in range(nk // KC):
            kk = key_s[cc * KC:(cc + 1) * KC, :]
            eq = kk == thr
            eqf = jnp.where(eq, 1.0, 0.0)
            rank = jnp.dot(tri, eqf.astype(BF16), preferred_element_type=F32) + run
            ok = ((kk > thr) | (eq & (rank <= need))) & (kk > KEY_NEG_INF) & (kk < KEY_POS_INF)
            bias_s[cc * KC:(cc + 1) * KC, :] = jnp.where(ok, 0.0, NEG_INF)
            run = run + jnp.sum(eqf, axis=0, keepdims=True)

    b = bias_s[0:nk, :]
    bias2 = jnp.concatenate([b, b], axis=1)
    zeros_blk = jnp.zeros((HEAD_DIM, 2 * Q_BLOCK), BF16)
    for g in range(B_KV_HEADS):
        pair = jnp.concatenate([q_t[(2 * g) * HEAD_DIM:(2 * g + 1) * HEAD_DIM, :],
                                q_t[(2 * g + 1) * HEAD_DIM:(2 * g + 2) * HEAD_DIM, :]], axis=1)
        rhs = jnp.concatenate([pair if gg == g else zeros_blk for gg in range(B_KV_HEADS)], axis=0)
        s = jnp.dot(kbf_s[0:nk, :], rhs, preferred_element_type=F32) + bias2
        mx = jnp.max(s, axis=0, keepdims=True)
        pr = jnp.exp(s - mx)
        lsum = jnp.sum(pr, axis=0, keepdims=True)
        acc = jnp.dot(vt_s[g * HEAD_DIM:(g + 1) * HEAD_DIM, 0:nk], pr.astype(BF16),
                      preferred_element_type=F32)
        o = acc / lsum
        ot_s[(2 * g) * HEAD_DIM:(2 * g + 1) * HEAD_DIM, :] = o[:, :Q_BLOCK]
        ot_s[(2 * g + 1) * HEAD_DIM:(2 * g + 2) * HEAD_DIM, :] = o[:, Q_BLOCK:]


def _dsa_prompt_kernel(q_ref, iq_ref, ikwq_ref, ikw_ref, k_ref, v_ref, y_ref,
                       kbf_s, vt_s, ikbf_s, key_s, bias_s, ot_s, *, T, classes):
    qb = pl.program_id(1)

    @pl.when(qb == 0)
    def _():
        kbf_s[...] = k_ref[0].astype(BF16)
        ikbf_s[...] = ikw_ref[0][:, :IDX_DIM].astype(BF16)
        for cc in range(T // KC):
            vt_s[:, cc * KC:(cc + 1) * KC] = v_ref[0, cc * KC:(cc + 1) * KC, :].T.astype(BF16)

    iq_t = iq_ref[0].T.astype(BF16)
    w_t = ikwq_ref[0].T[IDX_DIM:IDX_DIM + IDX_HEADS, :] * (IDX_DIM ** -0.5 * IDX_HEADS ** -0.5)
    q_t = (q_ref[0] * (HEAD_DIM ** -0.5)).T.astype(BF16)

    per_class = (T // Q_BLOCK) // classes
    for cls in range(classes):
        nk = (cls + 1) * per_class * Q_BLOCK

        @pl.when(qb // per_class == cls)
        def _(nk=nk):
            _dsa_prompt_block(nk, qb, iq_t, w_t, q_t, kbf_s, vt_s, ikbf_s, key_s, bias_s, ot_s)

    y_ref[0] = ot_s[...].T


def _dsa_prompt(q3, iq3, ikw3, k3, v3):
    bsz, t, _ = q3.shape
    nqb = t // Q_BLOCK
    blk = lambda b, j: (b, j, 0)
    full = lambda b, j: (b, 0, 0)
    classes = math.gcd(nqb, KEY_CLASSES)
    assert (nqb // classes) * Q_BLOCK % KC == 0
    kern = functools.partial(_dsa_prompt_kernel, T=t, classes=classes)
    return pl.pallas_call(
        kern,
        grid=(bsz, nqb),
        in_specs=[pl.BlockSpec((1, Q_BLOCK, B_WIDTH), blk),
                  pl.BlockSpec((1, Q_BLOCK, IQ_WIDTH), blk),
                  pl.BlockSpec((1, Q_BLOCK, LANES), blk),
                  pl.BlockSpec((1, t, LANES), full),
                  pl.BlockSpec((1, t, KV_WIDTH), full),
                  pl.BlockSpec((1, t, KV_WIDTH), full)],
        out_specs=pl.BlockSpec((1, Q_BLOCK, B_WIDTH), blk),
        out_shape=jax.ShapeDtypeStruct((bsz, t, B_WIDTH), F32),
        scratch_shapes=[pltpu.VMEM((t, KV_WIDTH), BF16),
                        pltpu.VMEM((KV_WIDTH, t), BF16),
                        pltpu.VMEM((t, IDX_DIM), BF16),
                        pltpu.VMEM((t, Q_BLOCK), I32),
                        pltpu.VMEM((t, Q_BLOCK), F32),
                        pltpu.VMEM((B_WIDTH, Q_BLOCK), F32)],
        compiler_params=_cparams("parallel", "arbitrary"),
    )(q3, iq3, ikw3, ikw3, k3, v3)


PG = 16


def _sidx_kernel(pt_ref, iqf_ref, wcol_ref, *rest, Ts):
    pages = rest[:PG]
    out_ref = rest[PG]
    iqf = iqf_ref[0].astype(BF16)
    wcol = wcol_ref[0]
    for i in range(PG):
        s = _dot(iqf, pages[i][0, 0])
        r = jnp.maximum(s, 0.0) * wcol
        out_ref[0, i] = jnp.sum(r.reshape(IDX_HEADS, Ts, PAGE), axis=0)


def _sidx(page_table, iqf, wcol, cache_idx_kt, l, Ts):
    db, n_pages = page_table.shape
    steps = n_pages // PG

    def page_spec(i):
        return pl.BlockSpec((1, 1, IDX_DIM, PAGE), lambda b, j, pt: (l, pt[b, j * PG + i], 0, 0))

    grid_spec = pltpu.PrefetchScalarGridSpec(
        num_scalar_prefetch=1,
        grid=(db, steps),
        in_specs=[pl.BlockSpec((1, IDX_HEADS * Ts, IDX_DIM), lambda b, j, pt: (b, 0, 0)),
                  pl.BlockSpec((1, IDX_HEADS * Ts, 1), lambda b, j, pt: (b, 0, 0))]
                 + [page_spec(i) for i in range(PG)],
        out_specs=pl.BlockSpec((1, PG, Ts, PAGE), lambda b, j, pt: (b, j, 0, 0)),
    )
    return pl.pallas_call(
        functools.partial(_sidx_kernel, Ts=Ts),
        grid_spec=grid_spec,
        out_shape=jax.ShapeDtypeStruct((db, n_pages, Ts, PAGE), F32),
        compiler_params=_cparams("parallel", "arbitrary"),
    )(page_table, iqf, wcol, *([cache_idx_kt] * PG))


def _ssel_kernel(sc_ref, iqf_ref, wcol_ref, iknew_ref, bias_ref, key_s, *, Ts, NT):
    key_s[0:NT] = _order_key(sc_ref[0])
    s = _dot_nt(iqf_ref[0], iknew_ref[0])
    r = jnp.maximum(s, 0.0) * wcol_ref[0]
    snew = jnp.sum(r.reshape(IDX_HEADS, Ts, PAGE), axis=0)
    vis = _iota((Ts, PAGE), 1) <= _iota((Ts, PAGE), 0)
    key_s[NT] = _order_key(jnp.where(vis, snew, NEG_INF))
    keys = key_s[...]

    def count(mask):
        part = jnp.sum(jnp.where(mask, 1.0, 0.0), axis=0)
        return jnp.sum(part, axis=1, keepdims=True)

    def bit_body(i, t):
        cand = t + jnp.left_shift(jnp.int32(1), 31 - i)
        cnt = count(keys >= cand[None])
        return jnp.where(cnt >= TOPK, cand, t)

    thr = lax.fori_loop(0, 32, bit_body, jnp.full((Ts, 1), INT_MIN, I32))
    need = TOPK - count(keys > thr[None])
    n_eq = count(keys == thr[None])
    finite = (keys > KEY_NEG_INF) & (keys < KEY_POS_INF)
    bias_ref[0] = jnp.where((keys >= thr[None]) & finite, 0.0, NEG_INF)
    has_tie = jnp.max(jnp.where((n_eq > need) & (thr > KEY_NEG_INF), 1.0, 0.0)) > 0.0

    @pl.when(has_tie)
    def _():
        triu = (_iota((PAGE, PAGE), 0) <= _iota((PAGE, PAGE), 1)).astype(BF16)

        def tile_body(j, run):
            kk = key_s[j]
            eq = kk == thr
            eqf = jnp.where(eq, 1.0, 0.0)
            rank = jnp.dot(eqf.astype(BF16), triu, preferred_element_type=F32) + run
            sel = (kk > thr) | (eq & (rank <= need))
            ok = sel & (kk > KEY_NEG_INF) & (kk < KEY_POS_INF)
            bias_ref[0, j] = jnp.where(ok, 0.0, NEG_INF)
            return run + jnp.sum(eqf, axis=1, keepdims=True)

        lax.fori_loop(0, NT + 1, tile_body, jnp.zeros((Ts, 1), F32))


def _ssel(scores, iqf, wcol, iknew):
    db, nt, ts, _ = scores.shape
    b3 = lambda b: (b, 0, 0)
    b4 = lambda b: (b, 0, 0, 0)
    return pl.pallas_call(
        functools.partial(_ssel_kernel, Ts=ts, NT=nt),
        grid=(db,),
        in_specs=[pl.BlockSpec((1, nt, ts, PAGE), b4),
                  pl.BlockSpec((1, IDX_HEADS * ts, IDX_DIM), b3),
                  pl.BlockSpec((1, IDX_HEADS * ts, 1), b3),
                  pl.BlockSpec((1, PAGE, IDX_DIM), b3)],
        out_specs=pl.BlockSpec((1, nt + 1, ts, PAGE), b4),
        out_shape=jax.ShapeDtypeStruct((db, nt + 1, ts, PAGE), F32),
        scratch_shapes=[pltpu.VMEM((nt + 1, ts, PAGE), I32)],
        compiler_params=_cparams("parallel"),
    )(scores, iqf, wcol, iknew)


def _sattn_kernel(pt_ref, qbd_ref, bias_ref, biasn_ref, knew_ref, vnew_ref, *rest, Ts):
    kpages = rest[:PG]
    vpages = rest[PG:2 * PG]
    y_ref = rest[2 * PG]
    m_s, l_s, acc_s = rest[2 * PG + 1:]
    j = pl.program_id(1)

    @pl.when(j == 0)
    def _():
        m_s[...] = jnp.full(m_s.shape, NEG_INF, F32)
        l_s[...] = jnp.zeros(l_s.shape, F32)
        acc_s[...] = jnp.zeros(acc_s.shape, F32)

    qbd = qbd_ref[0].astype(BF16)

    def rows(b):
        return jnp.concatenate([b] * B_HEADS, axis=0)

    def update(s, pv):
        m_old = m_s[...]
        m_new = jnp.maximum(m_old, jnp.max(s, axis=1, keepdims=True))
        m_safe = jnp.where(m_new == NEG_INF, 0.0, m_new)
        scale = jnp.exp(m_old - m_safe)
        pr = jnp.exp(s - m_safe)
        l_s[...] = scale * l_s[...] + jnp.sum(pr, axis=1, keepdims=True)
        acc_s[...] = scale * acc_s[...] + pv(pr)
        m_s[...] = m_new

    s = jnp.concatenate(
        [_dot(qbd, kpages[i][0, 0].reshape(KV_WIDTH, PAGE)) + rows(bias_ref[0, i]) for i in range(PG)],
        axis=1)

    def pv(pr):
        acc = jnp.zeros((B_HEADS * Ts, KV_WIDTH), F32)
        for i in range(PG):
            acc = acc + _dot_nt(pr[:, i * PAGE:(i + 1) * PAGE], vpages[i][0, 0].reshape(KV_WIDTH, PAGE))
        return acc

    update(s, pv)

    @pl.when(j == pl.num_programs(1) - 1)
    def _():
        sn = _dot(qbd, knew_ref[0]) + rows(biasn_ref[0, 0])
        update(sn, lambda pr: _dot_nt(pr, vnew_ref[0]))
        o = acc_s[...] / l_s[...]
        for h in range(B_HEADS):
            g = h // (B_HEADS // B_KV_HEADS)
            y_ref[0, :, h * HEAD_DIM:(h + 1) * HEAD_DIM] = o[h * Ts:(h + 1) * Ts, g * HEAD_DIM:(g + 1) * HEAD_DIM]


def _sattn(page_table, qbd, bias, knew_t, vnew_t, cache_kt, cache_vt, l, Ts):
    db, n_pages = page_table.shape
    steps = n_pages // PG

    def page_spec(i):
        return pl.BlockSpec((1, 1, B_KV_HEADS, HEAD_DIM, PAGE),
                            lambda b, j, pt: (l, pt[b, j * PG + i], 0, 0, 0))

    b3 = lambda b, j, pt: (b, 0, 0)
    nrows = B_HEADS * Ts
    grid_spec = pltpu.PrefetchScalarGridSpec(
        num_scalar_prefetch=1,
        grid=(db, steps),
        in_specs=[pl.BlockSpec((1, nrows, KV_WIDTH), b3),
                  pl.BlockSpec((1, PG, Ts, PAGE), lambda b, j, pt: (b, j, 0, 0)),
                  pl.BlockSpec((1, 1, Ts, PAGE), lambda b, j, pt: (b, n_pages, 0, 0)),
                  pl.BlockSpec((1, KV_WIDTH, PAGE), b3),
                  pl.BlockSpec((1, KV_WIDTH, PAGE), b3)]
                 + [page_spec(i) for i in range(PG)] + [page_spec(i) for i in range(PG)],
        out_specs=pl.BlockSpec((1, Ts, B_WIDTH), b3),
        scratch_shapes=[pltpu.VMEM((nrows, 1), F32),
                        pltpu.VMEM((nrows, 1), F32),
                        pltpu.VMEM((nrows, KV_WIDTH), F32)],
    )
    return pl.pallas_call(
        functools.partial(_sattn_kernel, Ts=Ts),
        grid_spec=grid_spec,
        out_shape=jax.ShapeDtypeStruct((db, Ts, B_WIDTH), F32),
        compiler_params=_cparams("parallel", "arbitrary"),
    )(page_table, qbd, bias, bias, knew_t, vnew_t, *([cache_kt] * PG), *([cache_vt] * PG))


def _dsa_sample(q3, iq3, ikw3, k3, v3, cache_kt, cache_vt, cache_idx_kt, page_table, l):
    db, ts, _ = q3.shape
    iqf = iq3.reshape(db, ts, IDX_HEADS, IDX_DIM).transpose(0, 2, 1, 3).reshape(db, IDX_HEADS * ts, IDX_DIM)
    iw = ikw3[:, :, IDX_DIM:IDX_DIM + IDX_HEADS] * (IDX_DIM ** -0.5 * IDX_HEADS ** -0.5)
    wcol = iw.transpose(0, 2, 1).reshape(db, IDX_HEADS * ts, 1)
    qh = (q3 * (HEAD_DIM ** -0.5)).reshape(db, ts, B_HEADS, HEAD_DIM).transpose(0, 2, 1, 3)
    own = (jnp.arange(B_HEADS)[:, None] // (B_HEADS // B_KV_HEADS)) == jnp.arange(B_KV_HEADS)[None, :]
    qbd = jnp.where(own[None, :, None, :, None], qh[:, :, :, None, :], 0.0)
    qbd = qbd.reshape(db, B_HEADS * ts, KV_WIDTH)
    pad = lambda a: jnp.pad(a, ((0, 0), (0, PAGE - ts), (0, 0)))
    iknew = pad(ikw3[:, :, :IDX_DIM])
    scores = _sidx(page_table, iqf, wcol, cache_idx_kt, l, ts)
    bias = _ssel(scores, iqf, wcol, iknew)
    knew_t = pad(k3).transpose(0, 2, 1)
    vnew_t = pad(v3).transpose(0, 2, 1)
    return _sattn(page_table, qbd, bias, knew_t, vnew_t, cache_kt, cache_vt, l, ts)


def _ffn_kernel(x_ref, ya_ref, yb_ref, yc_ref, wo_ref, gf_ref, wg_ref, wu_ref, wd_ref, gfin_ref,
                o_ref, *, final):
    mixed = (jnp.dot(ya_ref[...].astype(BF16), wo_ref[0:A_WIDTH, :], preferred_element_type=F32)
             + jnp.dot(yb_ref[...].astype(BF16), wo_ref[A_WIDTH:A_WIDTH + B_WIDTH, :],
                       preferred_element_type=F32)
             + jnp.dot(yc_ref[...].astype(BF16), wo_ref[A_WIDTH + B_WIDTH:, :],
                       preferred_element_type=F32))
    x1 = x_ref[...] + mixed
    ms = jnp.mean(x1 * x1, axis=-1, keepdims=True)
    hf = (x1 * lax.rsqrt(ms + NORM_EPS) * gf_ref[...]).astype(BF16)
    gate = jnp.dot(hf, wg_ref[...], preferred_element_type=F32)
    up = jnp.dot(hf, wu_ref[...], preferred_element_type=F32)
    act = (gate * _sigmoid(gate) * up).astype(BF16)
    x2 = x1 + jnp.dot(act, wd_ref[...], preferred_element_type=F32)
    if final:
        ms2 = jnp.mean(x2 * x2, axis=-1, keepdims=True)
        x2 = x2 * lax.rsqrt(ms2 + NORM_EPS) * gfin_ref[...]
    o_ref[...] = x2


def _ffn(x2d, ya, yb, yc, wo, gf, wg, wu, wd, gfin, final, tm):
    n, d = x2d.shape
    dff = wg.shape[1]
    row = lambda i: (i, 0)
    fixed = lambda i: (0, 0)
    once = dict(pipeline_mode=pl.Buffered(1))
    return pl.pallas_call(
        functools.partial(_ffn_kernel, final=final),
        grid=(n // tm,),
        in_specs=[pl.BlockSpec((tm, d), row), pl.BlockSpec((tm, A_WIDTH), row),
                  pl.BlockSpec((tm, B_WIDTH), row), pl.BlockSpec((tm, C_WIDTH), row),
                  pl.BlockSpec((d, d), fixed, **once), pl.BlockSpec((1, d), fixed),
                  pl.BlockSpec((d, dff), fixed, **once), pl.BlockSpec((d, dff), fixed, **once),
                  pl.BlockSpec((dff, d), fixed, **once), pl.BlockSpec((1, d), fixed)],
        out_specs=pl.BlockSpec((tm, d), row),
        out_shape=jax.ShapeDtypeStruct((n, d), F32),
        compiler_params=_cparams("parallel"),
    )(x2d, ya, yb, yc, wo, gf, wg, wu, wd, gfin)


def _rope_tables(pos):
    half = ROT_DIM // 2
    inv = ROPE_THETA ** (-jnp.arange(half, dtype=F32) / half)
    ang = pos.astype(F32)[:, None] * inv[None, :]
    j = jnp.arange(LANES) % HEAD_DIM
    cos = jnp.cos(ang)[:, j % half]
    sin = jnp.sin(ang)[:, j % half]
    cos_t = jnp.where(j[None, :] < ROT_DIM, cos, 1.0)
    sina = jnp.where(j[None, :] < half, -sin, 0.0)
    sinb = jnp.where((j[None, :] >= half) & (j[None, :] < ROT_DIM), sin, 0.0)
    return cos_t, sina, sinb


def _pad_w_in(w):
    d = w.shape[0]
    z = lambda n: jnp.zeros((d, n), w.dtype)
    return jnp.concatenate([w[:, :A_PROJ], z(A_PAD - A_PROJ),
                            w[:, A_PROJ:A_PROJ + B_PROJ], z(B_PAD - B_PROJ),
                            w[:, A_PROJ + B_PROJ:]], axis=1).astype(BF16)


def _layer(x3, l, tabs, mstate, shift0, s0, attn_fn, p, wts, final, tm):
    bsz, t, d = x3.shape
    n = bsz * t
    pa, q, k, v, iq, ikw, pc = _inproj(x3.reshape(n, d), p['norm_mix'][l].reshape(1, d), wts['w_in'][l], tabs, tm)
    r3 = lambda a: a.reshape(bsz, t, a.shape[-1])
    y_a, c_new, n_new, m_new = _mlstm(r3(pa), *mstate, p['mlstm_gate_b'][l], p['mlstm_norm'][l])
    y_b = attn_fn(r3(q), r3(iq), r3(ikw), r3(k), r3(v))
    y_c, shift, s_new = _rwkv(r3(pc), shift0, s0, p, l)
    x_new = _ffn(x3.reshape(n, d), y_a.reshape(n, -1), y_b.reshape(n, -1), y_c.reshape(n, -1),
                 wts['w_out'][l], p['norm_ffn'][l].reshape(1, d), wts['w_gate'][l], wts['w_up'][l],
                 wts['w_down'][l], p['norm_final'].reshape(1, d), final, tm)
    new = (k.reshape(bsz, t, B_KV_HEADS, HEAD_DIM), v.reshape(bsz, t, B_KV_HEADS, HEAD_DIM),
           r3(ikw)[:, :, :IDX_DIM], c_new, n_new, m_new, shift, s_new)
    return x_new.reshape(bsz, t, d), new


def kernel(x_prompt, x_sample, cache_k, cache_v, cache_idx_k, state_mlstm_C, state_mlstm_n, state_mlstm_m, state_rwkv_shift, state_rwkv_S, page_table, norm_mix, w_in, mlstm_gate_b, mlstm_norm, rwkv_mu, rwkv_w0, rwkv_w2, rwkv_a0, rwkv_a2, rwkv_g2, rwkv_k_k, rwkv_k_a, rwkv_r_k, rwkv_ln_w, rwkv_ln_b, w_out, norm_ffn, w_gate, w_up, w_down, norm_final):
    p = dict(norm_mix=norm_mix, mlstm_gate_b=mlstm_gate_b, mlstm_norm=mlstm_norm,
             rwkv_mu=rwkv_mu, rwkv_w0=rwkv_w0, rwkv_w2=rwkv_w2, rwkv_a0=rwkv_a0, rwkv_a2=rwkv_a2,
             rwkv_g2=rwkv_g2, rwkv_k_k=rwkv_k_k, rwkv_k_a=rwkv_k_a, rwkv_r_k=rwkv_r_k,
             rwkv_ln_w=rwkv_ln_w, rwkv_ln_b=rwkv_ln_b, norm_ffn=norm_ffn, norm_final=norm_final)
    depth = w_in.shape[0]
    wts = dict(w_in=[_pad_w_in(w_in[l]) for l in range(depth)],
               w_out=w_out.astype(BF16), w_gate=w_gate.astype(BF16),
               w_up=w_up.astype(BF16), w_down=w_down.astype(BF16))
    bp, tp, _ = x_prompt.shape
    db, ts, _ = x_sample.shape
    past = page_table.shape[1] * PAGE
    tabs_p = _rope_tables(jnp.arange(tp, dtype=I32))
    tabs_s = tuple(jnp.tile(a, (db, 1)) for a in _rope_tables(past + jnp.arange(ts, dtype=I32)))
    tm_p = min(256, bp * tp)
    tm_s = db * ts
    zc = jnp.zeros((bp, A_HEADS, HEAD_DIM, HEAD_DIM), F32)
    zn = jnp.zeros((bp, A_HEADS, HEAD_DIM), F32)
    zm = jnp.zeros((bp, A_HEADS), F32)
    zshift = jnp.zeros((bp, C_PROJ), F32)
    zs = jnp.zeros((bp, C_HEADS, HEAD_DIM, HEAD_DIM), F32)
    cache_kt = cache_k.transpose(0, 1, 3, 4, 2)
    cache_vt = cache_v.transpose(0, 1, 3, 4, 2)
    cache_idx_kt = cache_idx_k.transpose(0, 1, 3, 2)
    xp, xs = x_prompt, x_sample
    new_p, new_s = [], []
    for l in range(depth):
        final = l == depth - 1
        xp, st = _layer(xp, l, tabs_p, (zc, zn, zm), zshift, zs, _dsa_prompt, p, wts, final, tm_p)
        new_p.append(st)
        attn_s = functools.partial(_dsa_sample, cache_kt=cache_kt, cache_vt=cache_vt,
                                   cache_idx_kt=cache_idx_kt, page_table=page_table, l=l)
        xs, st = _layer(xs, l, tabs_s, (state_mlstm_C[l], state_mlstm_n[l], state_mlstm_m[l]),
                        state_rwkv_shift[l], state_rwkv_S[l], attn_s, p, wts, final, tm_s)
        new_s.append(st)
    stack = lambda states, i: jnp.stack([st[i] for st in states])
    outs_p = [stack(new_p, i) for i in range(8)]
    outs_s = [stack(new_s, i) for i in range(8)]
    return (xp, xs, *outs_p, *outs_s)
```

```python
import functools
import math

import jax
import jax.numpy as jnp
from jax import lax
from jax.experimental import pallas as pl
from jax.experimental.pallas import tpu as pltpu

F32 = jnp.float32
BF16 = jnp.bfloat16
I32 = jnp.int32

HEAD_DIM = 64
A_HEADS = 4
B_HEADS = 8
B_KV_HEADS = 4
C_HEADS = 4
IDX_HEADS = 8
IDX_DIM = 64
TOPK = 256
Q_BLOCK = 128
ROT_DIM = 16
ROPE_THETA = 500000.0
MLSTM_CHUNK = 64
RWKV_CHUNK = 64
SEQ_ROWS_PER_STEP = 4
RWKV_LN_EPS = 64e-5
NORM_EPS = 1e-6
PAGE = 128

A_WIDTH = A_HEADS * HEAD_DIM
B_WIDTH = B_HEADS * HEAD_DIM
KV_WIDTH = B_KV_HEADS * HEAD_DIM
C_WIDTH = C_HEADS * HEAD_DIM
IQ_WIDTH = IDX_HEADS * IDX_DIM
A_PROJ = 4 * A_WIDTH + 2 * A_HEADS
B_PROJ = B_WIDTH + 2 * KV_WIDTH + IQ_WIDTH + IDX_DIM + IDX_HEADS
C_PROJ = 3 * C_WIDTH + 64 + 64 + 128

LANES = 128
A_PAD = 6 * A_WIDTH
B_PAD = B_PROJ + (LANES - (IDX_DIM + IDX_HEADS))
P_PAD = A_PAD + B_PAD + C_PROJ

VMEM_LIMIT = 56 * 1024 * 1024

INT_MIN = -2147483648
KEY_NEG_INF = -2139095040
KEY_POS_INF = 2139095040
NEG_INF = float("-inf")
PAD_LOG_GATE = -1e30
SUBLANES = 8


def _cparams(*sem):
    return pltpu.CompilerParams(dimension_semantics=sem, vmem_limit_bytes=VMEM_LIMIT)


def _dot(a, b):
    return jnp.dot(a.astype(BF16), b.astype(BF16), preferred_element_type=F32)


def _dot_nt(a, b):
    return lax.dot_general(a.astype(BF16), b.astype(BF16), (((1,), (1,)), ((), ())),
                           preferred_element_type=F32)


def _dot_tn(a, b):
    return lax.dot_general(a.astype(BF16), b.astype(BF16), (((0,), (0,)), ((), ())),
                           preferred_element_type=F32)


def _split2(a):
    hi = a.astype(BF16)
    lo = (a - hi.astype(F32)).astype(BF16)
    return hi, lo


def _split3(a):
    hi = a.astype(BF16)
    r = a - hi.astype(F32)
    mid = r.astype(BF16)
    lo = (r - mid.astype(F32)).astype(BF16)
    return hi, mid, lo


_NN = (((1,), (0,)), ((), ()))
_NT = (((1,), (1,)), ((), ()))
_TN = (((0,), (0,)), ((), ()))


def _dg(a, b, dims):
    return lax.dot_general(a, b, dims, preferred_element_type=F32)


def _dot3(a, b, dims=_NN):
    ah, al = _split2(a)
    bh, bl = _split2(b)
    return _dg(ah, bh, dims) + (_dg(ah, bl, dims) + _dg(al, bh, dims))


def _dot01_left(m01, x):
    m = m01.astype(BF16)
    x1, x2, x3 = _split3(x)
    return _dg(m, x1, _NN) + (_dg(m, x2, _NN) + _dg(m, x3, _NN))


def _dot01_right(x, m01):
    m = m01.astype(BF16)
    x1, x2, x3 = _split3(x)
    return _dg(x1, m, _NN) + (_dg(x2, m, _NN) + _dg(x3, m, _NN))


def _sigmoid(x):
    return 1.0 / (1.0 + jnp.exp(-x))


def _softplus(x):
    return jnp.maximum(x, 0.0) + jnp.log1p(jnp.exp(-jnp.abs(x)))


def _iota(shape, dim):
    return lax.broadcasted_iota(I32, shape, dim)


def _reduce_rows(x, op):
    n, w = x.shape
    group = 8 * SUBLANES
    if n > group and n % group == 0:
        x = op(x.reshape(n // group, group, w), axis=0)
    return op(x, axis=0, keepdims=True)


def _rope_tile(xt, cos, sina, sinb):
    up = pltpu.roll(xt, LANES - ROT_DIM // 2, axis=1)
    dn = pltpu.roll(xt, ROT_DIM // 2, axis=1)
    return xt * cos + up * sina + dn * sinb


def _inproj_kernel(x_ref, g_ref, w_ref, cos_ref, sina_ref, sinb_ref,
                   pa_ref, q_ref, k_ref, v_ref, iq_ref, ikw_ref, pc_ref):
    x = x_ref[...]
    ms = jnp.mean(x * x, axis=-1, keepdims=True)
    h = (x * lax.rsqrt(ms + NORM_EPS) * g_ref[...]).astype(BF16)
    cos = cos_ref[...]
    sina = sina_ref[...]
    sinb = sinb_ref[...]

    def proj(lo, width):
        return jnp.dot(h, w_ref[:, lo:lo + width], preferred_element_type=F32)

    def rope(x2):
        tiles = [_rope_tile(x2[:, j * LANES:(j + 1) * LANES], cos, sina, sinb)
                 for j in range(x2.shape[1] // LANES)]
        return tiles[0] if len(tiles) == 1 else jnp.concatenate(tiles, axis=1)

    pa_ref[...] = proj(0, A_PAD)
    b0 = A_PAD
    q_ref[...] = rope(proj(b0, B_WIDTH))
    k_ref[...] = rope(proj(b0 + B_WIDTH, KV_WIDTH))
    v_ref[...] = proj(b0 + B_WIDTH + KV_WIDTH, KV_WIDTH)
    iq_ref[...] = rope(proj(b0 + B_WIDTH + 2 * KV_WIDTH, IQ_WIDTH))
    ikw = proj(b0 + B_WIDTH + 2 * KV_WIDTH + IQ_WIDTH, LANES)
    lane = _iota(ikw.shape, 1)
    ikw_ref[...] = jnp.where(lane < IDX_DIM, _rope_tile(ikw, cos, sina, sinb), ikw)
    pc_ref[...] = proj(A_PAD + B_PAD, C_PROJ)


def _inproj(x2d, g, w_bf, tabs, tm):
    n, d = x2d.shape
    cos, sina, sinb = tabs
    tab_blocks = cos.shape[0] // tm
    row = lambda i: (i, 0)
    fixed = lambda i: (0, 0)
    tab = lambda i: (i % tab_blocks, 0)
    widths = (A_PAD, B_WIDTH, KV_WIDTH, KV_WIDTH, IQ_WIDTH, LANES, C_PROJ)
    return pl.pallas_call(
        _inproj_kernel,
        grid=(n // tm,),
        in_specs=[pl.BlockSpec((tm, d), row), pl.BlockSpec((1, d), fixed),
                  pl.BlockSpec((d, P_PAD), fixed),
                  pl.BlockSpec((tm, LANES), tab), pl.BlockSpec((tm, LANES), tab),
                  pl.BlockSpec((tm, LANES), tab)],
        out_specs=[pl.BlockSpec((tm, w), row) for w in widths],
        out_shape=[jax.ShapeDtypeStruct((n, w), F32) for w in widths],
        compiler_params=_cparams("parallel"),
    )(x2d, g, w_bf, cos, sina, sinb)


def _mlstm_kernel(pa_ref, c0_ref, n0_ref, m0_ref, gb_ref, gn_ref,
                  y_ref, cout_ref, nout_ref, mout_ref, c_s, n_s, m_s, *, L, Lp, NB):
    c = pl.program_id(1)
    W = A_WIDTH
    HL = A_HEADS * Lp

    @pl.when(c == 0)
    def _():
        n_s[...] = n0_ref[...]
        m_s[...] = m0_ref[...]
        c_s[...] = jnp.zeros((NB, W, W), F32)
        for bb in range(NB):
            for h in range(A_HEADS):
                sl = slice(h * HEAD_DIM, (h + 1) * HEAD_DIM)
                c_s[bb, sl, sl] = c0_ref[bb, h]

    rowi = _iota((Lp, W), 0)
    valid = rowi < L
    tri = _iota((Lp, Lp), 0) >= _iota((Lp, Lp), 1)
    ones_ll = jnp.ones((Lp, Lp), F32)
    key_of_lane = _iota((Lp, HL), 1) % Lp
    eye_t = _iota((Lp, HL), 0) == key_of_lane
    causal_t = _iota((Lp, HL), 0) >= key_of_lane
    lane_h = _iota((Lp, W), 1) // HEAD_DIM
    same_head = _iota((W, W), 0) // HEAD_DIM == _iota((W, W), 1) // HEAD_DIM

    def stack(x):
        return jnp.concatenate([jnp.where(lane_h == h, x, 0.0) for h in range(A_HEADS)], axis=0)

    def head_sum(x):
        return _dot01_right(x, same_head)

    for bb in range(NB):
        pa = pa_ref[bb]
        if Lp > L:
            pa = jnp.concatenate([pa, jnp.zeros((Lp - L, pa.shape[1]), F32)], axis=0)
        q = pa[:, 0:W]
        k = pa[:, W:2 * W] * (HEAD_DIM ** -0.5)
        v = pa[:, 2 * W:3 * W]
        o = pa[:, 3 * W:4 * W]
        gi = pa[:, 4 * W:5 * W] + gb_ref[:, 0:W]
        gf = pa[:, 5 * W:6 * W] + gb_ref[:, W:2 * W]
        log_i = jnp.where(valid, gi, PAD_LOG_GATE)
        log_f = jnp.where(valid, -_softplus(-gf), 0.0)
        b = _dot01_left(tri, log_f)
        ib = log_i - b
        cm = ib
        sh = 1
        while sh < Lp:
            cm = jnp.where(rowi >= sh, jnp.maximum(cm, pltpu.roll(cm, sh, axis=0)), cm)
            sh *= 2
        m_prev = m_s[bb]
        mx = jnp.maximum(m_prev, cm)
        m_t = b + mx
        w_inter = jnp.exp(m_prev - mx)
        ib_row = _dot01_left(ones_ll, jnp.where(eye_t, ib, 0.0))
        pmat = jnp.where(causal_t, jnp.exp(ib_row - mx), 0.0)
        k_st = stack(k).astype(BF16)
        v_st = stack(v).astype(BF16)
        q_bf = q.astype(BF16)
        s_all = _dg(q_bf, k_st, _NT) * pmat
        c_bd = c_s[bb]
        n_row = n_s[bb]
        num = _dg(s_all.astype(BF16), v_st, _NN) + w_inter * _dg(q_bf, c_bd.astype(BF16), _NT)
        den = head_sum(s_all) + w_inter * head_sum(q * n_row)
        hh = num / jnp.maximum(jnp.abs(den), jnp.exp(-m_t))
        ms = head_sum(hh * hh) * (1.0 / HEAD_DIM)
        hn = hh * lax.rsqrt(ms + NORM_EPS) * gn_ref[...]
        y_ref[bb] = (_sigmoid(o) * hn)[:L]
        mx_last = mx[Lp - 1:Lp, :]
        wk = jnp.exp(ib - mx_last)
        dec = jnp.exp(m_prev - mx_last)
        c_upd = _dg((v * wk).astype(BF16), k.astype(BF16), _TN)
        c_s[bb] = dec * c_bd + jnp.where(same_head, c_upd, 0.0)
        n_s[bb] = dec * n_row + jnp.sum(k * wk, axis=0, keepdims=True)
        m_s[bb] = b[Lp - 1:Lp, :] + mx_last

    @pl.when(c == pl.num_programs(1) - 1)
    def _():
        nout_ref[...] = n_s[...]
        mout_ref[...] = m_s[...]
        for bb in range(NB):
            for h in range(A_HEADS):
                sl = slice(h * HEAD_DIM, (h + 1) * HEAD_DIM)
                cout_ref[bb, h] = c_s[bb, sl, sl]


def _mlstm(pa3, c0, n0, m0, gate_b, gnorm):
    bsz, t, _ = pa3.shape
    L = min(t, MLSTM_CHUNK)
    Lp = MLSTM_CHUNK
    nc = t // L
    assert Lp == HEAD_DIM
    NB = math.gcd(bsz, SEQ_ROWS_PER_STEP)
    gb = jnp.repeat(gate_b, HEAD_DIM).reshape(1, 2 * A_WIDTH)
    m0_rep = jnp.repeat(m0, HEAD_DIM, axis=1).reshape(bsz, 1, A_WIDTH)
    kern = functools.partial(_mlstm_kernel, L=L, Lp=Lp, NB=NB)
    st4 = lambda b, c: (b, 0, 0, 0)
    st3 = lambda b, c: (b, 0, 0)
    fixed = lambda b, c: (0, 0)
    y, cn, nn, mn = pl.pallas_call(
        kern,
        grid=(bsz // NB, nc),
        in_specs=[pl.BlockSpec((NB, L, A_PAD), lambda b, c: (b, c, 0)),
                  pl.BlockSpec((NB, A_HEADS, HEAD_DIM, HEAD_DIM), st4),
                  pl.BlockSpec((NB, 1, A_WIDTH), st3),
                  pl.BlockSpec((NB, 1, A_WIDTH), st3),
                  pl.BlockSpec((1, 2 * A_WIDTH), fixed),
                  pl.BlockSpec((1, A_WIDTH), fixed)],
        out_specs=[pl.BlockSpec((NB, L, A_WIDTH), lambda b, c: (b, c, 0)),
                   pl.BlockSpec((NB, A_HEADS, HEAD_DIM, HEAD_DIM), st4),
                   pl.BlockSpec((NB, 1, A_WIDTH), st3),
                   pl.BlockSpec((NB, 1, A_WIDTH), st3)],
        out_shape=[jax.ShapeDtypeStruct((bsz, t, A_WIDTH), F32),
                   jax.ShapeDtypeStruct((bsz, A_HEADS, HEAD_DIM, HEAD_DIM), F32),
                   jax.ShapeDtypeStruct((bsz, 1, A_WIDTH), F32),
                   jax.ShapeDtypeStruct((bsz, 1, A_WIDTH), F32)],
        scratch_shapes=[pltpu.VMEM((NB, A_WIDTH, A_WIDTH), F32),
                        pltpu.VMEM((NB, 1, A_WIDTH), F32),
                        pltpu.VMEM((NB, 1, A_WIDTH), F32)],
        compiler_params=_cparams("parallel", "arbitrary"),
    )(pa3, c0, n0.reshape(bsz, 1, A_WIDTH), m0_rep, gb, gnorm.reshape(1, A_WIDTH))
    return y, cn, nn.reshape(bsz, A_HEADS, HEAD_DIM), mn[:, 0, ::HEAD_DIM]


def _rwkv_kernel(pc_ref, shift0_ref, s0_ref, mu_ref, w0_ref, w2_ref, a0_ref, a2_ref, g2_ref,
                 kk_ref, ka_ref, rk_ref, lnw_ref, lnb_ref,
                 y_ref, shift_out_ref, s_out_ref, s_s, carry_s, *, L, C, NB):
    c = pl.program_id(1)
    last = c == pl.num_programs(1) - 1
    W = C_WIDTH
    HC = C_HEADS * C

    @pl.when(c == 0)
    def _():
        carry_s[...] = shift0_ref[...]
        s_s[...] = jnp.zeros((NB, W, W), F32)
        for bb in range(NB):
            for h in range(C_HEADS):
                sl = slice(h * HEAD_DIM, (h + 1) * HEAD_DIM)
                s_s[bb, sl, sl] = s0_ref[bb, h]

    rowc = _iota((C, C_PROJ), 0)
    lane_head = _iota((W, W), 0) // HEAD_DIM == _iota((W, W), 1) // HEAD_DIM
    tri = _iota((C, C), 0) >= _iota((C, C), 1)
    lane_h = _iota((C, W), 1) // HEAD_DIM
    vrow = _iota((C, W), 0) < L
    tt = _iota((HC, HC), 0) % C
    ii = _iota((HC, HC), 1) % C
    eye = (_iota((HC, HC), 0) == _iota((HC, HC), 1)).astype(F32)

    def head_sum(x):
        return _dot01_right(x, lane_head)

    def stack(x):
        return jnp.concatenate([jnp.where(lane_h == h, x, 0.0) for h in range(C_HEADS)], axis=0)

    def level_mask(s):
        return ((tt // s) % 2 == 1) & ((ii // s) % 2 == 0) & (tt // (2 * s) == ii // (2 * s))

    for bb in range(NB):
        pc = pc_ref[bb]
        if C > L:
            pc = jnp.concatenate([pc, jnp.zeros((C - L, pc.shape[1]), F32)], axis=0)
        prev = jnp.where(rowc == 0, carry_s[bb], pltpu.roll(pc, 1, axis=0))
        carry_s[bb] = pc[L - 1:L, :]
        pcs = pc + mu_ref[...] * (prev - pc)
        cr = pcs[:, 0:W]
        ck = pcs[:, W:2 * W]
        cv = pcs[:, 2 * W:3 * W]
        cwl = pcs[:, 3 * W:3 * W + 64]
        cal = pcs[:, 3 * W + 64:3 * W + 128]
        cgl = pcs[:, 3 * W + 128:3 * W + 256]
        wlog = -_softplus(-(w0_ref[...] + _dot(jnp.tanh(cwl), w2_ref[...]))) - 0.5
        lw = -jnp.exp(wlog)
        a = _sigmoid(a0_ref[...] + _dot(cal, a2_ref[...]))
        g = _dot(_sigmoid(cgl), g2_ref[...])
        kk = ck * kk_ref[...]
        kkn = kk / jnp.maximum(jnp.sqrt(head_sum(kk * kk)), 1e-12)
        k2 = ck * (1.0 + (a - 1.0) * ka_ref[...])
        alpha = -kkn
        beta = kkn * a
        if C > L:
            zero = lambda z: jnp.where(vrow, z, 0.0)
            lw, alpha, beta, k2s, cvs, crs = zero(lw), zero(alpha), zero(beta), zero(k2), zero(cv), zero(cr)
        else:
            k2s, cvs, crs = k2, cv, cr

        logp = _dot01_left(tri, lw)
        p = jnp.exp(logp)
        pinv = jnp.exp(-logp)
        pprev = jnp.exp(logp - lw)
        p_last = p[C - 1:C, :]
        a_st, r_st = stack(alpha * pprev), stack(crs * p)
        b_st, k_st, v_st = stack(beta * pinv), stack(k2s * pinv), stack(cvs)
        a_bf, b_bf, k_bf, v_bf = a_st.astype(BF16), b_st.astype(BF16), k_st.astype(BF16), v_st.astype(BF16)
        ar = jnp.concatenate([a_bf, r_st.astype(BF16)], axis=0)
        gb = _dg(ar, b_bf, _NT)
        gk = _dg(ar, k_bf, _NT)
        l_ab = jnp.where(tt > ii, gb[:HC], 0.0)
        l_rb = jnp.where(tt >= ii, gb[HC:], 0.0).astype(BF16)
        l_ak = jnp.where(tt > ii, gk[:HC], 0.0).astype(BF16)
        l_rk = jnp.where(tt >= ii, gk[HC:], 0.0).astype(BF16)

        x = eye + jnp.where(level_mask(1), l_ab, 0.0)
        s = 2
        while s < C:
            e = jnp.where(level_mask(s), l_ab, 0.0).astype(BF16)
            xb = x.astype(BF16)
            x = x + _dg(_dg(xb, e, _NN).astype(BF16), xb, _NN)
            s *= 2

        xb = x.astype(BF16)
        a_t = _dg(xb, a_bf, _NN)
        v_t = _dg(xb, _dg(l_ak, v_bf, _NN).astype(BF16), _NN)
        r_q = r_st + _dg(l_rb, a_t.astype(BF16), _NN)
        y0 = _dg(l_rb, v_t.astype(BF16), _NN) + _dg(l_rk, v_bf, _NN)
        s_old = s_s[bb]
        u = _dot3(a_t, s_old, _NT) + v_t
        y_st = _dg(r_q.astype(BF16), s_old.astype(BF16), _NT) + y0
        s_new = (s_old + _dot3(u, b_st, _TN) + _dot3(v_st, k_st, _TN)) * p_last
        s_s[bb] = s_new
        yc = y_st[0:C]
        for h in range(1, C_HEADS):
            yc = yc + y_st[h * C:(h + 1) * C]

        inv_d = 1.0 / HEAD_DIM
        mean = head_sum(yc) * inv_d
        dlt = yc - mean
        var = head_sum(dlt * dlt) * inv_d
        ycn = dlt * lax.rsqrt(var + RWKV_LN_EPS) * lnw_ref[...] + lnb_ref[...]
        bonus = head_sum(cr * k2 * rk_ref[...]) * cv
        y_ref[bb] = ((ycn + bonus) * g)[:L]

    @pl.when(last)
    def _():
        shift_out_ref[...] = carry_s[...]
        for bb in range(NB):
            for h in range(C_HEADS):
                sl = slice(h * HEAD_DIM, (h + 1) * HEAD_DIM)
                s_out_ref[bb, h] = s_s[bb, sl, sl]


def _rwkv(pc3, shift0, s0, p, l):
    bsz, t, _ = pc3.shape
    C = RWKV_CHUNK
    L = min(t, C)
    nc = t // L
    NB = math.gcd(bsz, SEQ_ROWS_PER_STEP)
    r1 = lambda a: a.reshape(1, -1)
    params = [r1(p['rwkv_mu'][l]), r1(p['rwkv_w0'][l]), p['rwkv_w2'][l], r1(p['rwkv_a0'][l]),
              p['rwkv_a2'][l], p['rwkv_g2'][l], r1(p['rwkv_k_k'][l]), r1(p['rwkv_k_a'][l]),
              r1(p['rwkv_r_k'][l]), r1(p['rwkv_ln_w'][l]), r1(p['rwkv_ln_b'][l])]
    fixed = lambda b, c: (0, 0)
    pspecs = [pl.BlockSpec(a.shape, fixed) for a in params]
    kern = functools.partial(_rwkv_kernel, L=L, C=C, NB=NB)
    y, shift, s_new = pl.pallas_call(
        kern,
        grid=(bsz // NB, nc),
        in_specs=[pl.BlockSpec((NB, L, C_PROJ), lambda b, c: (b, c, 0)),
                  pl.BlockSpec((NB, 1, C_PROJ), lambda b, c: (b, 0, 0)),
                  pl.BlockSpec((NB, C_HEADS, HEAD_DIM, HEAD_DIM), lambda b, c: (b, 0, 0, 0))] + pspecs,
        out_specs=[pl.BlockSpec((NB, L, C_WIDTH), lambda b, c: (b, c, 0)),
                   pl.BlockSpec((NB, 1, C_PROJ), lambda b, c: (b, 0, 0)),
                   pl.BlockSpec((NB, C_HEADS, HEAD_DIM, HEAD_DIM), lambda b, c: (b, 0, 0, 0))],
        out_shape=[jax.ShapeDtypeStruct((bsz, t, C_WIDTH), F32),
                   jax.ShapeDtypeStruct((bsz, 1, C_PROJ), F32),
                   jax.ShapeDtypeStruct((bsz, C_HEADS, HEAD_DIM, HEAD_DIM), F32)],
        scratch_shapes=[pltpu.VMEM((NB, C_WIDTH, C_WIDTH), F32), pltpu.VMEM((NB, 1, C_PROJ), F32)],
        compiler_params=_cparams("parallel", "arbitrary"),
    )(pc3, shift0.reshape(bsz, 1, C_PROJ), s0, *params)
    return y, shift.reshape(bsz, C_PROJ), s_new


def _order_key(score):
    bits = pltpu.bitcast(score, I32)
    return jnp.where(bits < 0, (bits ^ 0x7FFFFFFF) + 1, bits)


KC = 256
KEY_CLASSES = 4


def _dsa_prompt_block(nk, qb, iq_t, w_t, q_t, kbf_s, vt_s, ikbf_s, key_s, bias_s, ot_s):
    ik = ikbf_s[0:nk, :]
    acc = jnp.zeros((nk, Q_BLOCK), F32)
    for h in range(IDX_HEADS):
        st = jnp.dot(ik, iq_t[h * IDX_DIM:(h + 1) * IDX_DIM, :], preferred_element_type=F32)
        acc = acc + jnp.maximum(st, 0.0) * w_t[h:h + 1, :]
    visible = _iota((nk, Q_BLOCK), 0) <= qb * Q_BLOCK + _iota((nk, Q_BLOCK), 1)
    key_s[0:nk, :] = _order_key(jnp.where(visible, acc, NEG_INF))

    def count(pred):
        return _reduce_rows(jnp.where(pred(key_s[0:nk, :]), 1.0, 0.0), jnp.sum)

    def bit_body(i, t):
        cand = t + jnp.left_shift(jnp.int32(1), 31 - i)
        return jnp.where(count(lambda kk: kk >= cand) >= TOPK, cand, t)

    thr = lax.fori_loop(0, 32, bit_body, jnp.full((1, Q_BLOCK), INT_MIN, I32))
    need = TOPK - count(lambda kk: kk > thr)
    n_eq = count(lambda kk: kk == thr)
    keys = key_s[0:nk, :]
    floor = jnp.maximum(thr, KEY_NEG_INF + 1)
    bias_s[0:nk, :] = jnp.where((keys >= floor) & (keys < KEY_POS_INF), 0.0, NEG_INF)
    has_tie = jnp.max(jnp.where((n_eq > need) & (thr > KEY_NEG_INF), 1.0, 0.0)) > 0.0

    @pl.when(has_tie)
    def _():
        tri = (_iota((KC, KC), 0) >= _iota((KC, KC), 1)).astype(BF16)
        run = jnp.zeros((1, Q_BLOCK), F32)
        for cc in range(nk // KC):
            kk = key_s[cc * KC:(cc + 1) * KC, :]
            eq = kk == thr
            eqf = jnp.where(eq, 1.0, 0.0)
            rank = jnp.dot(tri, eqf.astype(BF16), preferred_element_type=F32) + run
            ok = ((kk > thr) | (eq & (rank <= need))) & (kk > KEY_NEG_INF) & (kk < KEY_POS_INF)
            bias_s[cc * KC:(cc + 1) * KC, :] = jnp.where(ok, 0.0, NEG_INF)
            run = run + jnp.sum(eqf, axis=0, keepdims=True)

    b = bias_s[0:nk, :]
    bias2 = jnp.concatenate([b, b], axis=1)
    zeros_blk = jnp.zeros((HEAD_DIM, 2 * Q_BLOCK), BF16)
    for g in range(B_KV_HEADS):
        pair = jnp.concatenate([q_t[(2 * g) * HEAD_DIM:(2 * g + 1) * HEAD_DIM, :],
                                q_t[(2 * g + 1) * HEAD_DIM:(2 * g + 2) * HEAD_DIM, :]], axis=1)
        rhs = jnp.concatenate([pair if gg == g else zeros_blk for gg in range(B_KV_HEADS)], axis=0)
        s = jnp.dot(kbf_s[0:nk, :], rhs, preferred_element_type=F32) + bias2
        mx = _reduce_rows(s, jnp.max)
        pr = jnp.exp(s - mx)
        lsum = _reduce_rows(pr, jnp.sum)
        acc = jnp.dot(vt_s[g * HEAD_DIM:(g + 1) * HEAD_DIM, 0:nk], pr.astype(BF16),
                      preferred_element_type=F32)
        o = acc / lsum
        ot_s[(2 * g) * HEAD_DIM:(2 * g + 1) * HEAD_DIM, :] = o[:, :Q_BLOCK]
        ot_s[(2 * g + 1) * HEAD_DIM:(2 * g + 2) * HEAD_DIM, :] = o[:, Q_BLOCK:]


def _dsa_prompt_kernel(q_ref, iq_ref, ikwq_ref, ikw_ref, k_ref, v_ref, y_ref,
                       kbf_s, vt_s, ikbf_s, key_s, bias_s, ot_s, *, T, classes):
    qb = pl.program_id(1)

    @pl.when(qb == 0)
    def _():
        kbf_s[...] = k_ref[0].astype(BF16)
        ikbf_s[...] = ikw_ref[0][:, :IDX_DIM].astype(BF16)
        for cc in range(T // KC):
            vt_s[:, cc * KC:(cc + 1) * KC] = v_ref[0, cc * KC:(cc + 1) * KC, :].T.astype(BF16)

    iq_t = iq_ref[0].T.astype(BF16)
    w_t = ikwq_ref[0].T[IDX_DIM:IDX_DIM + IDX_HEADS, :] * (IDX_DIM ** -0.5 * IDX_HEADS ** -0.5)
    q_t = (q_ref[0] * (HEAD_DIM ** -0.5)).T.astype(BF16)

    per_class = (T // Q_BLOCK) // classes
    for cls in range(classes):
        nk = (cls + 1) * per_class * Q_BLOCK

        @pl.when(qb // per_class == cls)
        def _(nk=nk):
            _dsa_prompt_block(nk, qb, iq_t, w_t, q_t, kbf_s, vt_s, ikbf_s, key_s, bias_s, ot_s)

    y_ref[0] = ot_s[...].T


def _dsa_prompt(q3, iq3, ikw3, k3, v3):
    bsz, t, _ = q3.shape
    nqb = t // Q_BLOCK
    blk = lambda b, j: (b, j, 0)
    full = lambda b, j: (b, 0, 0)
    classes = math.gcd(nqb, KEY_CLASSES)
    assert (nqb // classes) * Q_BLOCK % KC == 0
    kern = functools.partial(_dsa_prompt_kernel, T=t, classes=classes)
    return pl.pallas_call(
        kern,
        grid=(bsz, nqb),
        in_specs=[pl.BlockSpec((1, Q_BLOCK, B_WIDTH), blk),
                  pl.BlockSpec((1, Q_BLOCK, IQ_WIDTH), blk),
                  pl.BlockSpec((1, Q_BLOCK, LANES), blk),
                  pl.BlockSpec((1, t, LANES), full),
                  pl.BlockSpec((1, t, KV_WIDTH), full),
                  pl.BlockSpec((1, t, KV_WIDTH), full)],
        out_specs=pl.BlockSpec((1, Q_BLOCK, B_WIDTH), blk),
        out_shape=jax.ShapeDtypeStruct((bsz, t, B_WIDTH), F32),
        scratch_shapes=[pltpu.VMEM((t, KV_WIDTH), BF16),
                        pltpu.VMEM((KV_WIDTH, t), BF16),
                        pltpu.VMEM((t, IDX_DIM), BF16),
                        pltpu.VMEM((t, Q_BLOCK), I32),
                        pltpu.VMEM((t, Q_BLOCK), F32),
                        pltpu.VMEM((B_WIDTH, Q_BLOCK), F32)],
        compiler_params=_cparams("parallel", "arbitrary"),
    )(q3, iq3, ikw3, ikw3, k3, v3)


PG = 16


def _sidx_kernel(pt_ref, iqf_ref, wcol_ref, *rest, Ts):
    pages = rest[:PG]
    out_ref = rest[PG]
    iqf = iqf_ref[0].astype(BF16)
    wcol = wcol_ref[0]
    for i in range(PG):
        s = _dot(iqf, pages[i][0, 0])
        r = jnp.maximum(s, 0.0) * wcol
        out_ref[0, i] = jnp.sum(r.reshape(IDX_HEADS, Ts, PAGE), axis=0)


def _sidx(page_table, iqf, wcol, cache_idx_kt, l, Ts):
    db, n_pages = page_table.shape
    steps = n_pages // PG

    def page_spec(i):
        return pl.BlockSpec((1, 1, IDX_DIM, PAGE), lambda b, j, pt: (l, pt[b, j * PG + i], 0, 0))

    grid_spec = pltpu.PrefetchScalarGridSpec(
        num_scalar_prefetch=1,
        grid=(db, steps),
        in_specs=[pl.BlockSpec((1, IDX_HEADS * Ts, IDX_DIM), lambda b, j, pt: (b, 0, 0)),
                  pl.BlockSpec((1, IDX_HEADS * Ts, 1), lambda b, j, pt: (b, 0, 0))]
                 + [page_spec(i) for i in range(PG)],
        out_specs=pl.BlockSpec((1, PG, Ts, PAGE), lambda b, j, pt: (b, j, 0, 0)),
    )
    return pl.pallas_call(
        functools.partial(_sidx_kernel, Ts=Ts),
        grid_spec=grid_spec,
        out_shape=jax.ShapeDtypeStruct((db, n_pages, Ts, PAGE), F32),
        compiler_params=_cparams("parallel", "arbitrary"),
    )(page_table, iqf, wcol, *([cache_idx_kt] * PG))


def _ssel_kernel(sc_ref, iqf_ref, wcol_ref, iknew_ref, bias_ref, key_s, *, Ts, NT):
    key_s[0:NT] = _order_key(sc_ref[0])
    s = _dot_nt(iqf_ref[0], iknew_ref[0])
    r = jnp.maximum(s, 0.0) * wcol_ref[0]
    snew = jnp.sum(r.reshape(IDX_HEADS, Ts, PAGE), axis=0)
    vis = _iota((Ts, PAGE), 1) <= _iota((Ts, PAGE), 0)
    key_s[NT] = _order_key(jnp.where(vis, snew, NEG_INF))
    keys = key_s[...]

    def count(mask):
        ones = jnp.where(mask, 1.0, 0.0)
        part = ones[NT]
        if NT % 8 == 0:
            part = part + jnp.sum(jnp.sum(ones[0:NT].reshape(NT // 8, 8, Ts, PAGE), axis=0), axis=0)
        else:
            part = part + jnp.sum(ones[0:NT], axis=0)
        return jnp.sum(part, axis=1, keepdims=True)

    def bit_body(i, t):
        cand = t + jnp.left_shift(jnp.int32(1), 31 - i)
        cnt = count(keys >= cand[None])
        return jnp.where(cnt >= TOPK, cand, t)

    thr = lax.fori_loop(0, 32, bit_body, jnp.full((Ts, 1), INT_MIN, I32))
    need = TOPK - count(keys > thr[None])
    n_eq = count(keys == thr[None])
    finite = (keys > KEY_NEG_INF) & (keys < KEY_POS_INF)
    bias_ref[0] = jnp.where((keys >= thr[None]) & finite, 0.0, NEG_INF)
    has_tie = jnp.max(jnp.where((n_eq > need) & (thr > KEY_NEG_INF), 1.0, 0.0)) > 0.0

    @pl.when(has_tie)
    def _():
        triu = (_iota((PAGE, PAGE), 0) <= _iota((PAGE, PAGE), 1)).astype(BF16)

        def tile_body(j, run):
            kk = key_s[j]
            eq = kk == thr
            eqf = jnp.where(eq, 1.0, 0.0)
            rank = jnp.dot(eqf.astype(BF16), triu, preferred_element_type=F32) + run
            sel = (kk > thr) | (eq & (rank <= need))
            ok = sel & (kk > KEY_NEG_INF) & (kk < KEY_POS_INF)
            bias_ref[0, j] = jnp.where(ok, 0.0, NEG_INF)
            return run + jnp.sum(eqf, axis=1, keepdims=True)

        lax.fori_loop(0, NT + 1, tile_body, jnp.zeros((Ts, 1), F32))


def _ssel(scores, iqf, wcol, iknew):
    db, nt, ts, _ = scores.shape
    b3 = lambda b: (b, 0, 0)
    b4 = lambda b: (b, 0, 0, 0)
    return pl.pallas_call(
        functools.partial(_ssel_kernel, Ts=ts, NT=nt),
        grid=(db,),
        in_specs=[pl.BlockSpec((1, nt, ts, PAGE), b4),
                  pl.BlockSpec((1, IDX_HEADS * ts, IDX_DIM), b3),
                  pl.BlockSpec((1, IDX_HEADS * ts, 1), b3),
                  pl.BlockSpec((1, PAGE, IDX_DIM), b3)],
        out_specs=pl.BlockSpec((1, nt + 1, ts, PAGE), b4),
        out_shape=jax.ShapeDtypeStruct((db, nt + 1, ts, PAGE), F32),
        scratch_shapes=[pltpu.VMEM((nt + 1, ts, PAGE), I32)],
        compiler_params=_cparams("parallel"),
    )(scores, iqf, wcol, iknew)


def _sattn_kernel(pt_ref, qbd_ref, bias_ref, biasn_ref, knew_ref, vnew_ref, *rest, Ts):
    kpages = rest[:PG]
    vpages = rest[PG:2 * PG]
    y_ref = rest[2 * PG]
    m_s, l_s, acc_s = rest[2 * PG + 1:]
    j = pl.program_id(1)

    @pl.when(j == 0)
    def _():
        m_s[...] = jnp.full(m_s.shape, NEG_INF, F32)
        l_s[...] = jnp.zeros(l_s.shape, F32)
        acc_s[...] = jnp.zeros(acc_s.shape, F32)

    qbd = qbd_ref[0].astype(BF16)

    def rows(b):
        return jnp.concatenate([b] * B_HEADS, axis=0)

    def update(s, pv):
        m_old = m_s[...]
        m_new = jnp.maximum(m_old, jnp.max(s, axis=1, keepdims=True))
        m_safe = jnp.where(m_new == NEG_INF, 0.0, m_new)
        scale = jnp.exp(m_old - m_safe)
        pr = jnp.exp(s - m_safe)
        l_s[...] = scale * l_s[...] + jnp.sum(pr, axis=1, keepdims=True)
        acc_s[...] = scale * acc_s[...] + pv(pr)
        m_s[...] = m_new

    s = jnp.concatenate(
        [_dot(qbd, kpages[i][0, 0].reshape(KV_WIDTH, PAGE)) + rows(bias_ref[0, i]) for i in range(PG)],
        axis=1)

    def pv(pr):
        acc = jnp.zeros((B_HEADS * Ts, KV_WIDTH), F32)
        for i in range(PG):
            acc = acc + _dot_nt(pr[:, i * PAGE:(i + 1) * PAGE], vpages[i][0, 0].reshape(KV_WIDTH, PAGE))
        return acc

    update(s, pv)

    @pl.when(j == pl.num_programs(1) - 1)
    def _():
        sn = _dot(qbd, knew_ref[0]) + rows(biasn_ref[0, 0])
        update(sn, lambda pr: _dot_nt(pr, vnew_ref[0]))
        o = acc_s[...] / l_s[...]
        for h in range(B_HEADS):
            g = h // (B_HEADS // B_KV_HEADS)
            y_ref[0, :, h * HEAD_DIM:(h + 1) * HEAD_DIM] = o[h * Ts:(h + 1) * Ts, g * HEAD_DIM:(g + 1) * HEAD_DIM]


def _sattn(page_table, qbd, bias, knew_t, vnew_t, cache_kt, cache_vt, l, Ts):
    db, n_pages = page_table.shape
    steps = n_pages // PG

    def page_spec(i):
        return pl.BlockSpec((1, 1, B_KV_HEADS, HEAD_DIM, PAGE),
                            lambda b, j, pt: (l, pt[b, j * PG + i], 0, 0, 0))

    b3 = lambda b, j, pt: (b, 0, 0)
    nrows = B_HEADS * Ts
    grid_spec = pltpu.PrefetchScalarGridSpec(
        num_scalar_prefetch=1,
        grid=(db, steps),
        in_specs=[pl.BlockSpec((1, nrows, KV_WIDTH), b3),
                  pl.BlockSpec((1, PG, Ts, PAGE), lambda b, j, pt: (b, j, 0, 0)),
                  pl.BlockSpec((1, 1, Ts, PAGE), lambda b, j, pt: (b, n_pages, 0, 0)),
                  pl.BlockSpec((1, KV_WIDTH, PAGE), b3),
                  pl.BlockSpec((1, KV_WIDTH, PAGE), b3)]
                 + [page_spec(i) for i in range(PG)] + [page_spec(i) for i in range(PG)],
        out_specs=pl.BlockSpec((1, Ts, B_WIDTH), b3),
        scratch_shapes=[pltpu.VMEM((nrows, 1), F32),
                        pltpu.VMEM((nrows, 1), F32),
                        pltpu.VMEM((nrows, KV_WIDTH), F32)],
    )
    return pl.pallas_call(
        functools.partial(_sattn_kernel, Ts=Ts),
        grid_spec=grid_spec,
        out_shape=jax.ShapeDtypeStruct((db, Ts, B_WIDTH), F32),
        compiler_params=_cparams("parallel", "arbitrary"),
    )(page_table, qbd, bias, bias, knew_t, vnew_t, *([cache_kt] * PG), *([cache_vt] * PG))


def _dsa_sample(q3, iq3, ikw3, k3, v3, cache_kt, cache_vt, cache_idx_kt, page_table, l):
    db, ts, _ = q3.shape
    iqf = iq3.reshape(db, ts, IDX_HEADS, IDX_DIM).transpose(0, 2, 1, 3).reshape(db, IDX_HEADS * ts, IDX_DIM)
    iw = ikw3[:, :, IDX_DIM:IDX_DIM + IDX_HEADS] * (IDX_DIM ** -0.5 * IDX_HEADS ** -0.5)
    wcol = iw.transpose(0, 2, 1).reshape(db, IDX_HEADS * ts, 1)
    qh = (q3 * (HEAD_DIM ** -0.5)).reshape(db, ts, B_HEADS, HEAD_DIM).transpose(0, 2, 1, 3)
    own = (jnp.arange(B_HEADS)[:, None] // (B_HEADS // B_KV_HEADS)) == jnp.arange(B_KV_HEADS)[None, :]
    qbd = jnp.where(own[None, :, None, :, None], qh[:, :, :, None, :], 0.0)
    qbd = qbd.reshape(db, B_HEADS * ts, KV_WIDTH)
    pad = lambda a: jnp.pad(a, ((0, 0), (0, PAGE - ts), (0, 0)))
    iknew = pad(ikw3[:, :, :IDX_DIM])
    scores = _sidx(page_table, iqf, wcol, cache_idx_kt, l, ts)
    bias = _ssel(scores, iqf, wcol, iknew)
    knew_t = pad(k3).transpose(0, 2, 1)
    vnew_t = pad(v3).transpose(0, 2, 1)
    return _sattn(page_table, qbd, bias, knew_t, vnew_t, cache_kt, cache_vt, l, ts)


def _ffn_kernel(x_ref, ya_ref, yb_ref, yc_ref, wo_ref, gf_ref, wg_ref, wu_ref, wd_ref, gfin_ref,
                o_ref, *, final):
    mixed = (jnp.dot(ya_ref[...].astype(BF16), wo_ref[0:A_WIDTH, :], preferred_element_type=F32)
             + jnp.dot(yb_ref[...].astype(BF16), wo_ref[A_WIDTH:A_WIDTH + B_WIDTH, :],
                       preferred_element_type=F32)
             + jnp.dot(yc_ref[...].astype(BF16), wo_ref[A_WIDTH + B_WIDTH:, :],
                       preferred_element_type=F32))
    x1 = x_ref[...] + mixed
    ms = jnp.mean(x1 * x1, axis=-1, keepdims=True)
    hf = (x1 * lax.rsqrt(ms + NORM_EPS) * gf_ref[...]).astype(BF16)
    gate = jnp.dot(hf, wg_ref[...], preferred_element_type=F32)
    up = jnp.dot(hf, wu_ref[...], preferred_element_type=F32)
    act = (gate * _sigmoid(gate) * up).astype(BF16)
    x2 = x1 + jnp.dot(act, wd_ref[...], preferred_element_type=F32)
    if final:
        ms2 = jnp.mean(x2 * x2, axis=-1, keepdims=True)
        x2 = x2 * lax.rsqrt(ms2 + NORM_EPS) * gfin_ref[...]
    o_ref[...] = x2


def _ffn(x2d, ya, yb, yc, wo, gf, wg, wu, wd, gfin, final, tm):
    n, d = x2d.shape
    dff = wg.shape[1]
    row = lambda i: (i, 0)
    fixed = lambda i: (0, 0)
    once = dict(pipeline_mode=pl.Buffered(1))
    return pl.pallas_call(
        functools.partial(_ffn_kernel, final=final),
        grid=(n // tm,),
        in_specs=[pl.BlockSpec((tm, d), row), pl.BlockSpec((tm, A_WIDTH), row),
                  pl.BlockSpec((tm, B_WIDTH), row), pl.BlockSpec((tm, C_WIDTH), row),
                  pl.BlockSpec((d, d), fixed, **once), pl.BlockSpec((1, d), fixed),
                  pl.BlockSpec((d, dff), fixed, **once), pl.BlockSpec((d, dff), fixed, **once),
                  pl.BlockSpec((dff, d), fixed, **once), pl.BlockSpec((1, d), fixed)],
        out_specs=pl.BlockSpec((tm, d), row),
        out_shape=jax.ShapeDtypeStruct((n, d), F32),
        compiler_params=_cparams("parallel"),
    )(x2d, ya, yb, yc, wo, gf, wg, wu, wd, gfin)


def _rope_tables(pos):
    half = ROT_DIM // 2
    inv = ROPE_THETA ** (-jnp.arange(half, dtype=F32) / half)
    ang = pos.astype(F32)[:, None] * inv[None, :]
    j = jnp.arange(LANES) % HEAD_DIM
    cos = jnp.cos(ang)[:, j % half]
    sin = jnp.sin(ang)[:, j % half]
    cos_t = jnp.where(j[None, :] < ROT_DIM, cos, 1.0)
    sina = jnp.where(j[None, :] < half, -sin, 0.0)
    sinb = jnp.where((j[None, :] >= half) & (j[None, :] < ROT_DIM), sin, 0.0)
    return cos_t, sina, sinb


def _pad_w_in(w):
    d = w.shape[0]
    z = lambda n: jnp.zeros((d, n), w.dtype)
    gates = jnp.repeat(w[:, 4 * A_WIDTH:A_PROJ], HEAD_DIM, axis=1)
    return jnp.concatenate([w[:, :4 * A_WIDTH], gates,
                            w[:, A_PROJ:A_PROJ + B_PROJ], z(B_PAD - B_PROJ),
                            w[:, A_PROJ + B_PROJ:]], axis=1).astype(BF16)


def _layer(x3, l, tabs, mstate, shift0, s0, attn_fn, p, wts, final, tm):
    bsz, t, d = x3.shape
    n = bsz * t
    pa, q, k, v, iq, ikw, pc = _inproj(x3.reshape(n, d), p['norm_mix'][l].reshape(1, d), wts['w_in'][l], tabs, tm)
    r3 = lambda a: a.reshape(bsz, t, a.shape[-1])
    y_a, c_new, n_new, m_new = _mlstm(r3(pa), *mstate, p['mlstm_gate_b'][l], p['mlstm_norm'][l])
    y_b = attn_fn(r3(q), r3(iq), r3(ikw), r3(k), r3(v))
    y_c, shift, s_new = _rwkv(r3(pc), shift0, s0, p, l)
    x_new = _ffn(x3.reshape(n, d), y_a.reshape(n, -1), y_b.reshape(n, -1), y_c.reshape(n, -1),
                 wts['w_out'][l], p['norm_ffn'][l].reshape(1, d), wts['w_gate'][l], wts['w_up'][l],
                 wts['w_down'][l], p['norm_final'].reshape(1, d), final, tm)
    new = (k.reshape(bsz, t, B_KV_HEADS, HEAD_DIM), v.reshape(bsz, t, B_KV_HEADS, HEAD_DIM),
           r3(ikw)[:, :, :IDX_DIM], c_new, n_new, m_new, shift, s_new)
    return x_new.reshape(bsz, t, d), new


def kernel(x_prompt, x_sample, cache_k, cache_v, cache_idx_k, state_mlstm_C, state_mlstm_n, state_mlstm_m, state_rwkv_shift, state_rwkv_S, page_table, norm_mix, w_in, mlstm_gate_b, mlstm_norm, rwkv_mu, rwkv_w0, rwkv_w2, rwkv_a0, rwkv_a2, rwkv_g2, rwkv_k_k, rwkv_k_a, rwkv_r_k, rwkv_ln_w, rwkv_ln_b, w_out, norm_ffn, w_gate, w_up, w_down, norm_final):
    p = dict(norm_mix=norm_mix, mlstm_gate_b=mlstm_gate_b, mlstm_norm=mlstm_norm,
             rwkv_mu=rwkv_mu, rwkv_w0=rwkv_w0, rwkv_w2=rwkv_w2, rwkv_a0=rwkv_a0, rwkv_a2=rwkv_a2,
             rwkv_g2=rwkv_g2, rwkv_k_k=rwkv_k_k, rwkv_k_a=rwkv_k_a, rwkv_r_k=rwkv_r_k,
             rwkv_ln_w=rwkv_ln_w, rwkv_ln_b=rwkv_ln_b, norm_ffn=norm_ffn, norm_final=norm_final)
    depth = w_in.shape[0]
    wts = dict(w_in=[_pad_w_in(w_in[l]) for l in range(depth)],
               w_out=w_out.astype(BF16), w_gate=w_gate.astype(BF16),
               w_up=w_up.astype(BF16), w_down=w_down.astype(BF16))
    bp, tp, _ = x_prompt.shape
    db, ts, _ = x_sample.shape
    past = page_table.shape[1] * PAGE
    tabs_p = _rope_tables(jnp.arange(tp, dtype=I32))
    tabs_s = tuple(jnp.tile(a, (db, 1)) for a in _rope_tables(past + jnp.arange(ts, dtype=I32)))
    tm_p = min(256, bp * tp)
    tm_s = db * ts
    zc = jnp.zeros((bp, A_HEADS, HEAD_DIM, HEAD_DIM), F32)
    zn = jnp.zeros((bp, A_HEADS, HEAD_DIM), F32)
    zm = jnp.zeros((bp, A_HEADS), F32)
    zshift = jnp.zeros((bp, C_PROJ), F32)
    zs = jnp.zeros((bp, C_HEADS, HEAD_DIM, HEAD_DIM), F32)
    cache_kt = cache_k.transpose(0, 1, 3, 4, 2)
    cache_vt = cache_v.transpose(0, 1, 3, 4, 2)
    cache_idx_kt = cache_idx_k.transpose(0, 1, 3, 2)
    xp, xs = x_prompt, x_sample
    new_p, new_s = [], []
    for l in range(depth):
        final = l == depth - 1
        xp, st = _layer(xp, l, tabs_p, (zc, zn, zm), zshift, zs, _dsa_prompt, p, wts, final, tm_p)
        new_p.append(st)
        attn_s = functools.partial(_dsa_sample, cache_kt=cache_kt, cache_vt=cache_vt,
                                   cache_idx_kt=cache_idx_kt, page_table=page_table, l=l)
        xs, st = _layer(xs, l, tabs_s, (state_mlstm_C[l], state_mlstm_n[l], state_mlstm_m[l]),
                        state_rwkv_shift[l], state_rwkv_S[l], attn_s, p, wts, final, tm_s)
        new_s.append(st)
    stack = lambda states, i: jnp.stack([st[i] for st in states])
    outs_p = [stack(new_p, i) for i in range(8)]
    outs_s = [stack(new_s, i) for i in range(8)]
    return (xp, xs, *outs_p, *outs_s)
```

```python
import functools
import math

import jax
import jax.numpy as jnp
from jax import lax
from jax.experimental import pallas as pl
from jax.experimental.pallas import tpu as pltpu

F32 = jnp.float32
BF16 = jnp.bfloat16
I32 = jnp.int32

HEAD_DIM = 64
A_HEADS = 4
B_HEADS = 8
B_KV_HEADS = 4
C_HEADS = 4
IDX_HEADS = 8
IDX_DIM = 64
TOPK = 256
Q_BLOCK = 128
ROT_DIM = 16
ROPE_THETA = 500000.0
MLSTM_CHUNK = 64
RWKV_CHUNK = 64
SEQ_ROWS_PER_STEP = 4
RWKV_LN_EPS = 64e-5
NORM_EPS = 1e-6
PAGE = 128

A_WIDTH = A_HEADS * HEAD_DIM
B_WIDTH = B_HEADS * HEAD_DIM
KV_WIDTH = B_KV_HEADS * HEAD_DIM
C_WIDTH = C_HEADS * HEAD_DIM
IQ_WIDTH = IDX_HEADS * IDX_DIM
A_PROJ = 4 * A_WIDTH + 2 * A_HEADS
B_PROJ = B_WIDTH + 2 * KV_WIDTH + IQ_WIDTH + IDX_DIM + IDX_HEADS
C_PROJ = 3 * C_WIDTH + 64 + 64 + 128

LANES = 128
A_PAD = 6 * A_WIDTH
B_PAD = B_PROJ + (LANES - (IDX_DIM + IDX_HEADS))
P_PAD = A_PAD + B_PAD + C_PROJ

VMEM_LIMIT = 56 * 1024 * 1024

INT_MIN = -2147483648
KEY_NEG_INF = -2139095040
KEY_POS_INF = 2139095040
NEG_INF = float("-inf")
PAD_LOG_GATE = -1e30
SUBLANES = 8


def _cparams(*sem):
    return pltpu.CompilerParams(dimension_semantics=sem, vmem_limit_bytes=VMEM_LIMIT)


def _dot(a, b):
    return jnp.dot(a.astype(BF16), b.astype(BF16), preferred_element_type=F32)


def _dot_nt(a, b):
    return lax.dot_general(a.astype(BF16), b.astype(BF16), (((1,), (1,)), ((), ())),
                           preferred_element_type=F32)


def _split3(a):
    hi = a.astype(BF16)
    r = a - hi.astype(F32)
    mid = r.astype(BF16)
    lo = (r - mid.astype(F32)).astype(BF16)
    return hi, mid, lo


_NN = (((1,), (0,)), ((), ()))
_NT = (((1,), (1,)), ((), ()))
_TN = (((0,), (0,)), ((), ()))


def _dg(a, b, dims):
    return lax.dot_general(a, b, dims, preferred_element_type=F32)


def _dot01_left(m01, x):
    m = m01.astype(BF16)
    x1, x2, x3 = _split3(x)
    return _dg(m, x1, _NN) + (_dg(m, x2, _NN) + _dg(m, x3, _NN))


def _dot01_right(x, m01):
    m = m01.astype(BF16)
    x1, x2, x3 = _split3(x)
    return _dg(x1, m, _NN) + (_dg(x2, m, _NN) + _dg(x3, m, _NN))


def _sigmoid(x):
    return 1.0 / (1.0 + jnp.exp(-x))


def _softplus(x):
    return jnp.maximum(x, 0.0) + jnp.log1p(jnp.exp(-jnp.abs(x)))


def _iota(shape, dim):
    return lax.broadcasted_iota(I32, shape, dim)


def _round_robin(gens):
    gens = list(gens)
    while gens:
        alive = []
        for g in gens:
            try:
                next(g)
                alive.append(g)
            except StopIteration:
                pass
        gens = alive


def _reduce_rows(x, op):
    n, w = x.shape
    group = 8 * SUBLANES
    if n > group and n % group == 0:
        x = op(x.reshape(n // group, group, w), axis=0)
    return op(x, axis=0, keepdims=True)


def _rope_tile(xt, cos, sina, sinb):
    up = pltpu.roll(xt, LANES - ROT_DIM // 2, axis=1)
    dn = pltpu.roll(xt, ROT_DIM // 2, axis=1)
    return xt * cos + up * sina + dn * sinb


def _inproj_kernel(x_ref, g_ref, w_ref, cos_ref, sina_ref, sinb_ref,
                   pa_ref, q_ref, k_ref, v_ref, iq_ref, ikw_ref, pc_ref):
    x = x_ref[...]
    ms = jnp.mean(x * x, axis=-1, keepdims=True)
    h = (x * lax.rsqrt(ms + NORM_EPS) * g_ref[...]).astype(BF16)
    cos = cos_ref[...]
    sina = sina_ref[...]
    sinb = sinb_ref[...]

    def proj(lo, width):
        return jnp.dot(h, w_ref[:, lo:lo + width], preferred_element_type=F32)

    def rope(x2):
        tiles = [_rope_tile(x2[:, j * LANES:(j + 1) * LANES], cos, sina, sinb)
                 for j in range(x2.shape[1] // LANES)]
        return tiles[0] if len(tiles) == 1 else jnp.concatenate(tiles, axis=1)

    pa_ref[...] = proj(0, A_PAD)
    b0 = A_PAD
    q_ref[...] = rope(proj(b0, B_WIDTH))
    k_ref[...] = rope(proj(b0 + B_WIDTH, KV_WIDTH))
    v_ref[...] = proj(b0 + B_WIDTH + KV_WIDTH, KV_WIDTH)
    iq_ref[...] = rope(proj(b0 + B_WIDTH + 2 * KV_WIDTH, IQ_WIDTH))
    ikw = proj(b0 + B_WIDTH + 2 * KV_WIDTH + IQ_WIDTH, LANES)
    lane = _iota(ikw.shape, 1)
    ikw_ref[...] = jnp.where(lane < IDX_DIM, _rope_tile(ikw, cos, sina, sinb), ikw)
    pc_ref[...] = proj(A_PAD + B_PAD, C_PROJ)


def _inproj(x2d, g, w_bf, tabs, tm):
    n, d = x2d.shape
    cos, sina, sinb = tabs
    tab_blocks = cos.shape[0] // tm
    row = lambda i: (i, 0)
    fixed = lambda i: (0, 0)
    tab = lambda i: (i % tab_blocks, 0)
    widths = (A_PAD, B_WIDTH, KV_WIDTH, KV_WIDTH, IQ_WIDTH, LANES, C_PROJ)
    return pl.pallas_call(
        _inproj_kernel,
        grid=(n // tm,),
        in_specs=[pl.BlockSpec((tm, d), row), pl.BlockSpec((1, d), fixed),
                  pl.BlockSpec((d, P_PAD), fixed),
                  pl.BlockSpec((tm, LANES), tab), pl.BlockSpec((tm, LANES), tab),
                  pl.BlockSpec((tm, LANES), tab)],
        out_specs=[pl.BlockSpec((tm, w), row) for w in widths],
        out_shape=[jax.ShapeDtypeStruct((n, w), F32) for w in widths],
        compiler_params=_cparams("parallel"),
    )(x2d, g, w_bf, cos, sina, sinb)


def _mlstm_kernel(pa_ref, c0_ref, n0_ref, m0_ref, gb_ref, gn_ref,
                  y_ref, cout_ref, nout_ref, mout_ref, c_s, n_s, m_s, *, L, Lp, NB):
    c = pl.program_id(1)
    W = A_WIDTH
    HL = A_HEADS * Lp

    @pl.when(c == 0)
    def _():
        n_s[...] = n0_ref[...]
        m_s[...] = m0_ref[...]
        c_s[...] = jnp.zeros((NB, W, W), F32)
        for bb in range(NB):
            for h in range(A_HEADS):
                sl = slice(h * HEAD_DIM, (h + 1) * HEAD_DIM)
                c_s[bb, sl, sl] = c0_ref[bb, h]

    rowi = _iota((Lp, W), 0)
    valid = rowi < L
    tri = _iota((Lp, Lp), 0) >= _iota((Lp, Lp), 1)
    ones_ll = jnp.ones((Lp, Lp), F32)
    key_of_lane = _iota((Lp, HL), 1) % Lp
    eye_t = _iota((Lp, HL), 0) == key_of_lane
    causal_t = _iota((Lp, HL), 0) >= key_of_lane
    lane_h = _iota((Lp, W), 1) // HEAD_DIM
    same_head = _iota((W, W), 0) // HEAD_DIM == _iota((W, W), 1) // HEAD_DIM

    def stack(x):
        return jnp.concatenate([jnp.where(lane_h == h, x, 0.0) for h in range(A_HEADS)], axis=0)

    def head_sum(x):
        return _dot01_right(x, same_head)

    def row(bb):
        pa = pa_ref[bb]
        if Lp > L:
            pa = jnp.concatenate([pa, jnp.zeros((Lp - L, pa.shape[1]), F32)], axis=0)
        q = pa[:, 0:W]
        k = pa[:, W:2 * W] * (HEAD_DIM ** -0.5)
        v = pa[:, 2 * W:3 * W]
        o = pa[:, 3 * W:4 * W]
        gi = pa[:, 4 * W:5 * W] + gb_ref[:, 0:W]
        gf = pa[:, 5 * W:6 * W] + gb_ref[:, W:2 * W]
        log_i = jnp.where(valid, gi, PAD_LOG_GATE)
        log_f = jnp.where(valid, -_softplus(-gf), 0.0)
        k_st = stack(k).astype(BF16)
        v_st = stack(v).astype(BF16)
        q_bf = q.astype(BF16)
        yield
        b = _dot01_left(tri, log_f)
        qk = _dg(q_bf, k_st, _NT)
        c_bd = c_s[bb]
        n_row = n_s[bb]
        qc = _dg(q_bf, c_bd.astype(BF16), _NT)
        qn = head_sum(q * n_row)
        yield
        ib = log_i - b
        cm = ib
        sh = 1
        while sh < Lp:
            cm = jnp.where(rowi >= sh, jnp.maximum(cm, pltpu.roll(cm, sh, axis=0)), cm)
            sh *= 2
        m_prev = m_s[bb]
        mx = jnp.maximum(m_prev, cm)
        m_t = b + mx
        w_inter = jnp.exp(m_prev - mx)
        ib_diag = jnp.where(eye_t, ib, 0.0)
        mx_last = mx[Lp - 1:Lp, :]
        wk = jnp.exp(ib - mx_last)
        dec = jnp.exp(m_prev - mx_last)
        vw = (v * wk).astype(BF16)
        yield
        ib_row = _dot01_left(ones_ll, ib_diag)
        c_upd = _dg(vw, k.astype(BF16), _TN)
        yield
        c_s[bb] = dec * c_bd + jnp.where(same_head, c_upd, 0.0)
        n_s[bb] = dec * n_row + jnp.sum(k * wk, axis=0, keepdims=True)
        m_s[bb] = b[Lp - 1:Lp, :] + mx_last
        pmat = jnp.where(causal_t, jnp.exp(ib_row - mx), 0.0)
        s_all = qk * pmat
        s_bf = s_all.astype(BF16)
        yield
        num = _dg(s_bf, v_st, _NN) + w_inter * qc
        den = head_sum(s_all) + w_inter * qn
        yield
        hh = num / jnp.maximum(jnp.abs(den), jnp.exp(-m_t))
        hsq = hh * hh
        yield
        ms = head_sum(hsq) * (1.0 / HEAD_DIM)
        yield
        hn = hh * lax.rsqrt(ms + NORM_EPS) * gn_ref[...]
        y_ref[bb] = (_sigmoid(o) * hn)[:L]

    _round_robin([row(bb) for bb in range(NB)])

    @pl.when(c == pl.num_programs(1) - 1)
    def _():
        nout_ref[...] = n_s[...]
        mout_ref[...] = m_s[...]
        for bb in range(NB):
            for h in range(A_HEADS):
                sl = slice(h * HEAD_DIM, (h + 1) * HEAD_DIM)
                cout_ref[bb, h] = c_s[bb, sl, sl]


def _mlstm(pa3, c0, n0, m0, gate_b, gnorm):
    bsz, t, _ = pa3.shape
    L = min(t, MLSTM_CHUNK)
    Lp = MLSTM_CHUNK
    nc = t // L
    assert Lp == HEAD_DIM
    NB = math.gcd(bsz, SEQ_ROWS_PER_STEP)
    gb = jnp.repeat(gate_b, HEAD_DIM).reshape(1, 2 * A_WIDTH)
    m0_rep = jnp.repeat(m0, HEAD_DIM, axis=1).reshape(bsz, 1, A_WIDTH)
    kern = functools.partial(_mlstm_kernel, L=L, Lp=Lp, NB=NB)
    st4 = lambda b, c: (b, 0, 0, 0)
    st3 = lambda b, c: (b, 0, 0)
    fixed = lambda b, c: (0, 0)
    y, cn, nn, mn = pl.pallas_call(
        kern,
        grid=(bsz // NB, nc),
        in_specs=[pl.BlockSpec((NB, L, A_PAD), lambda b, c: (b, c, 0)),
                  pl.BlockSpec((NB, A_HEADS, HEAD_DIM, HEAD_DIM), st4),
                  pl.BlockSpec((NB, 1, A_WIDTH), st3),
                  pl.BlockSpec((NB, 1, A_WIDTH), st3),
                  pl.BlockSpec((1, 2 * A_WIDTH), fixed),
                  pl.BlockSpec((1, A_WIDTH), fixed)],
        out_specs=[pl.BlockSpec((NB, L, A_WIDTH), lambda b, c: (b, c, 0)),
                   pl.BlockSpec((NB, A_HEADS, HEAD_DIM, HEAD_DIM), st4),
                   pl.BlockSpec((NB, 1, A_WIDTH), st3),
                   pl.BlockSpec((NB, 1, A_WIDTH), st3)],
        out_shape=[jax.ShapeDtypeStruct((bsz, t, A_WIDTH), F32),
                   jax.ShapeDtypeStruct((bsz, A_HEADS, HEAD_DIM, HEAD_DIM), F32),
                   jax.ShapeDtypeStruct((bsz, 1, A_WIDTH), F32),
                   jax.ShapeDtypeStruct((bsz, 1, A_WIDTH), F32)],
        scratch_shapes=[pltpu.VMEM((NB, A_WIDTH, A_WIDTH), F32),
                        pltpu.VMEM((NB, 1, A_WIDTH), F32),
                        pltpu.VMEM((NB, 1, A_WIDTH), F32)],
        compiler_params=_cparams("parallel", "arbitrary"),
    )(pa3, c0, n0.reshape(bsz, 1, A_WIDTH), m0_rep, gb, gnorm.reshape(1, A_WIDTH))
    return y, cn, nn.reshape(bsz, A_HEADS, HEAD_DIM), mn[:, 0, ::HEAD_DIM]


def _rwkv_kernel(pc_ref, shift0_ref, s0_ref, mu_ref, w0_ref, w2_ref, a0_ref, a2_ref, g2_ref,
                 kk_ref, ka_ref, rk_ref, lnw_ref, lnb_ref,
                 y_ref, shift_out_ref, s_out_ref, s_s, carry_s, *, L, C, NB):
    c = pl.program_id(1)
    last = c == pl.num_programs(1) - 1
    W = C_WIDTH
    HC = C_HEADS * C

    @pl.when(c == 0)
    def _():
        carry_s[...] = shift0_ref[...]
        s_s[...] = jnp.zeros((NB, W, W), F32)
        for bb in range(NB):
            for h in range(C_HEADS):
                sl = slice(h * HEAD_DIM, (h + 1) * HEAD_DIM)
                s_s[bb, sl, sl] = s0_ref[bb, h]

    rowc = _iota((C, C_PROJ), 0)
    lane_head = _iota((W, W), 0) // HEAD_DIM == _iota((W, W), 1) // HEAD_DIM
    tri = _iota((C, C), 0) >= _iota((C, C), 1)
    lane_h = _iota((C, W), 1) // HEAD_DIM
    vrow = _iota((C, W), 0) < L
    tt = _iota((HC, HC), 0) % C
    ii = _iota((HC, HC), 1) % C
    eye = (_iota((HC, HC), 0) == _iota((HC, HC), 1)).astype(F32)

    def head_sum(x):
        return _dot01_right(x, lane_head)

    def stack(x):
        return jnp.concatenate([jnp.where(lane_h == h, x, 0.0) for h in range(C_HEADS)], axis=0)

    def level_mask(s):
        return ((tt // s) % 2 == 1) & ((ii // s) % 2 == 0) & (tt // (2 * s) == ii // (2 * s))

    levels = []
    s = 1
    while s < C:
        levels.append(level_mask(s))
        s *= 2

    def row(bb):
        pc = pc_ref[bb]
        if C > L:
            pc = jnp.concatenate([pc, jnp.zeros((C - L, pc.shape[1]), F32)], axis=0)
        prev = jnp.where(rowc == 0, carry_s[bb], pltpu.roll(pc, 1, axis=0))
        carry_s[bb] = pc[L - 1:L, :]
        pcs = pc + mu_ref[...] * (prev - pc)
        cr = pcs[:, 0:W]
        ck = pcs[:, W:2 * W]
        cv = pcs[:, 2 * W:3 * W]
        cwl = pcs[:, 3 * W:3 * W + 64]
        cal = pcs[:, 3 * W + 64:3 * W + 128]
        cgl = pcs[:, 3 * W + 128:3 * W + 256]
        wlog = -_softplus(-(w0_ref[...] + _dot(jnp.tanh(cwl), w2_ref[...]))) - 0.5
        lw = -jnp.exp(wlog)
        a = _sigmoid(a0_ref[...] + _dot(cal, a2_ref[...]))
        g = _dot(_sigmoid(cgl), g2_ref[...])
        kk = ck * kk_ref[...]
        kkn = kk / jnp.maximum(jnp.sqrt(head_sum(kk * kk)), 1e-12)
        k2 = ck * (1.0 + (a - 1.0) * ka_ref[...])
        alpha = -kkn
        beta = kkn * a
        if C > L:
            zero = lambda z: jnp.where(vrow, z, 0.0)
            lw, alpha, beta, k2s, cvs, crs = zero(lw), zero(alpha), zero(beta), zero(k2), zero(cv), zero(cr)
        else:
            k2s, cvs, crs = k2, cv, cr

        logp = _dot01_left(tri, lw)
        p = jnp.exp(logp)
        pinv = jnp.exp(-logp)
        pprev = jnp.exp(logp - lw)
        p_last = p[C - 1:C, :]
        a_st, r_st = stack(alpha * pprev), stack(crs * p)
        b_st, k_st, v_st = stack(beta * pinv), stack(k2s * pinv), stack(cvs)
        a_bf, b_bf, k_bf, v_bf = a_st.astype(BF16), b_st.astype(BF16), k_st.astype(BF16), v_st.astype(BF16)
        ar = jnp.concatenate([a_bf, r_st.astype(BF16)], axis=0)
        yield
        gb = _dg(ar, b_bf, _NT)
        gk = _dg(ar, k_bf, _NT)
        kv = _dg(v_bf, k_bf, _TN)
        yield
        l_ab = jnp.where(tt > ii, gb[:HC], 0.0)
        l_rb = jnp.where(tt >= ii, gb[HC:], 0.0).astype(BF16)
        l_ak = jnp.where(tt > ii, gk[:HC], 0.0).astype(BF16)
        l_rk = jnp.where(tt >= ii, gk[HC:], 0.0).astype(BF16)
        lakv = _dg(l_ak, v_bf, _NN).astype(BF16)
        lrkv = _dg(l_rk, v_bf, _NN)

        x = eye + jnp.where(levels[0], l_ab, 0.0)
        for lvl in levels[1:]:
            e = jnp.where(lvl, l_ab, 0.0).astype(BF16)
            xb = x.astype(BF16)
            yield
            xe = _dg(xb, e, _NN).astype(BF16)
            yield
            x = x + _dg(xe, xb, _NN)

        xb = x.astype(BF16)
        yield
        a_t = _dg(xb, a_bf, _NN)
        v_t = _dg(xb, lakv, _NN)
        yield
        s_old = s_s[bb]
        s_bf = s_old.astype(BF16)
        a_t_bf = a_t.astype(BF16)
        u = _dg(a_t_bf, s_bf, _NT) + v_t
        r_q = (r_st + _dg(l_rb, a_t_bf, _NN)).astype(BF16)
        y0 = _dg(l_rb, v_t.astype(BF16), _NN) + lrkv
        yield
        y_st = _dg(r_q, s_bf, _NT) + y0
        s_new = (s_old + _dg(u.astype(BF16), b_bf, _TN) + kv) * p_last
        s_s[bb] = s_new
        yield
        yc = y_st[0:C]
        for h in range(1, C_HEADS):
            yc = yc + y_st[h * C:(h + 1) * C]

        inv_d = 1.0 / HEAD_DIM
        mean = head_sum(yc) * inv_d
        dlt = yc - mean
        var = head_sum(dlt * dlt) * inv_d
        ycn = dlt * lax.rsqrt(var + RWKV_LN_EPS) * lnw_ref[...] + lnb_ref[...]
        bonus = head_sum(cr * k2 * rk_ref[...]) * cv
        y_ref[bb] = ((ycn + bonus) * g)[:L]

    _round_robin([row(bb) for bb in range(NB)])

    @pl.when(last)
    def _():
        shift_out_ref[...] = carry_s[...]
        for bb in range(NB):
            for h in range(C_HEADS):
                sl = slice(h * HEAD_DIM, (h + 1) * HEAD_DIM)
                s_out_ref[bb, h] = s_s[bb, sl, sl]


def _rwkv(pc3, shift0, s0, p, l):
    bsz, t, _ = pc3.shape
    C = RWKV_CHUNK
    L = min(t, C)
    nc = t // L
    NB = math.gcd(bsz, SEQ_ROWS_PER_STEP)
    r1 = lambda a: a.reshape(1, -1)
    params = [r1(p['rwkv_mu'][l]), r1(p['rwkv_w0'][l]), p['rwkv_w2'][l], r1(p['rwkv_a0'][l]),
              p['rwkv_a2'][l], p['rwkv_g2'][l], r1(p['rwkv_k_k'][l]), r1(p['rwkv_k_a'][l]),
              r1(p['rwkv_r_k'][l]), r1(p['rwkv_ln_w'][l]), r1(p['rwkv_ln_b'][l])]
    fixed = lambda b, c: (0, 0)
    pspecs = [pl.BlockSpec(a.shape, fixed) for a in params]
    kern = functools.partial(_rwkv_kernel, L=L, C=C, NB=NB)
    y, shift, s_new = pl.pallas_call(
        kern,
        grid=(bsz // NB, nc),
        in_specs=[pl.BlockSpec((NB, L, C_PROJ), lambda b, c: (b, c, 0)),
                  pl.BlockSpec((NB, 1, C_PROJ), lambda b, c: (b, 0, 0)),
                  pl.BlockSpec((NB, C_HEADS, HEAD_DIM, HEAD_DIM), lambda b, c: (b, 0, 0, 0))] + pspecs,
        out_specs=[pl.BlockSpec((NB, L, C_WIDTH), lambda b, c: (b, c, 0)),
                   pl.BlockSpec((NB, 1, C_PROJ), lambda b, c: (b, 0, 0)),
                   pl.BlockSpec((NB, C_HEADS, HEAD_DIM, HEAD_DIM), lambda b, c: (b, 0, 0, 0))],
        out_shape=[jax.ShapeDtypeStruct((bsz, t, C_WIDTH), F32),
                   jax.ShapeDtypeStruct((bsz, 1, C_PROJ), F32),
                   jax.ShapeDtypeStruct((bsz, C_HEADS, HEAD_DIM, HEAD_DIM), F32)],
        scratch_shapes=[pltpu.VMEM((NB, C_WIDTH, C_WIDTH), F32), pltpu.VMEM((NB, 1, C_PROJ), F32)],
        compiler_params=_cparams("parallel", "arbitrary"),
    )(pc3, shift0.reshape(bsz, 1, C_PROJ), s0, *params)
    return y, shift.reshape(bsz, C_PROJ), s_new


def _order_key(score):
    bits = pltpu.bitcast(score, I32)
    return jnp.where(bits < 0, (bits ^ 0x7FFFFFFF) + 1, bits)


KC = 256
KEY_CLASSES = 4


def _dsa_prompt_block(nk, qb, iq_t, w_t, q_t, kbf_s, vt_s, ikbf_s, key_s, bias_s, ot_s):
    ik = ikbf_s[0:nk, :]
    acc = jnp.zeros((nk, Q_BLOCK), F32)
    for h in range(IDX_HEADS):
        st = jnp.dot(ik, iq_t[h * IDX_DIM:(h + 1) * IDX_DIM, :], preferred_element_type=F32)
        acc = acc + jnp.maximum(st, 0.0) * w_t[h:h + 1, :]
    visible = _iota((nk, Q_BLOCK), 0) <= qb * Q_BLOCK + _iota((nk, Q_BLOCK), 1)
    key_s[0:nk, :] = _order_key(jnp.where(visible, acc, NEG_INF))

    def count(pred):
        return _reduce_rows(jnp.where(pred(key_s[0:nk, :]), 1.0, 0.0), jnp.sum)

    def bit_body(i, t):
        cand = t + jnp.left_shift(jnp.int32(1), 31 - i)
        return jnp.where(count(lambda kk: kk >= cand) >= TOPK, cand, t)

    thr = lax.fori_loop(0, 32, bit_body, jnp.full((1, Q_BLOCK), INT_MIN, I32))
    need = TOPK - count(lambda kk: kk > thr)
    n_eq = count(lambda kk: kk == thr)
    keys = key_s[0:nk, :]
    floor = jnp.maximum(thr, KEY_NEG_INF + 1)
    bias_s[0:nk, :] = jnp.where((keys >= floor) & (keys < KEY_POS_INF), 0.0, NEG_INF)
    has_tie = jnp.max(jnp.where((n_eq > need) & (thr > KEY_NEG_INF), 1.0, 0.0)) > 0.0

    @pl.when(has_tie)
    def _():
        tri = (_iota((KC, KC), 0) >= _iota((KC, KC), 1)).astype(BF16)
        run = jnp.zeros((1, Q_BLOCK), F32)
        for cc in range(nk // KC):
            kk = key_s[cc * KC:(cc + 1) * KC, :]
            eq = kk == thr
            eqf = jnp.where(eq, 1.0, 0.0)
            rank = jnp.dot(tri, eqf.astype(BF16), preferred_element_type=F32) + run
            ok = ((kk > thr) | (eq & (rank <= need))) & (kk > KEY_NEG_INF) & (kk < KEY_POS_INF)
            bias_s[cc * KC:(cc + 1) * KC, :] = jnp.where(ok, 0.0, NEG_INF)
            run = run + jnp.sum(eqf, axis=0, keepdims=True)

    b = bias_s[0:nk, :]
    bias2 = jnp.concatenate([b, b], axis=1)
    zeros_blk = jnp.zeros((HEAD_DIM, 2 * Q_BLOCK), BF16)
    for g in range(B_KV_HEADS):
        pair = jnp.concatenate([q_t[(2 * g) * HEAD_DIM:(2 * g + 1) * HEAD_DIM, :],
                                q_t[(2 * g + 1) * HEAD_DIM:(2 * g + 2) * HEAD_DIM, :]], axis=1)
        rhs = jnp.concatenate([pair if gg == g else zeros_blk for gg in range(B_KV_HEADS)], axis=0)
        s = jnp.dot(kbf_s[0:nk, :], rhs, preferred_element_type=F32) + bias2
        mx = _reduce_rows(s, jnp.max)
        pr = jnp.exp(s - mx)
        lsum = _reduce_rows(pr, jnp.sum)
        acc = jnp.dot(vt_s[g * HEAD_DIM:(g + 1) * HEAD_DIM, 0:nk], pr.astype(BF16),
                      preferred_element_type=F32)
        o = acc / lsum
        ot_s[(2 * g) * HEAD_DIM:(2 * g + 1) * HEAD_DIM, :] = o[:, :Q_BLOCK]
        ot_s[(2 * g + 1) * HEAD_DIM:(2 * g + 2) * HEAD_DIM, :] = o[:, Q_BLOCK:]


def _dsa_prompt_kernel(q_ref, iq_ref, ikwq_ref, ikw_ref, k_ref, v_ref, y_ref,
                       kbf_s, vt_s, ikbf_s, key_s, bias_s, ot_s, *, T, classes):
    qb = pl.program_id(1)

    @pl.when(qb == 0)
    def _():
        kbf_s[...] = k_ref[0].astype(BF16)
        ikbf_s[...] = ikw_ref[0][:, :IDX_DIM].astype(BF16)
        for cc in range(T // KC):
            vt_s[:, cc * KC:(cc + 1) * KC] = v_ref[0, cc * KC:(cc + 1) * KC, :].T.astype(BF16)

    iq_t = iq_ref[0].T.astype(BF16)
    w_t = ikwq_ref[0].T[IDX_DIM:IDX_DIM + IDX_HEADS, :] * (IDX_DIM ** -0.5 * IDX_HEADS ** -0.5)
    q_t = (q_ref[0] * (HEAD_DIM ** -0.5)).T.astype(BF16)

    per_class = (T // Q_BLOCK) // classes
    for cls in range(classes):
        nk = (cls + 1) * per_class * Q_BLOCK

        @pl.when(qb // per_class == cls)
        def _(nk=nk):
            _dsa_prompt_block(nk, qb, iq_t, w_t, q_t, kbf_s, vt_s, ikbf_s, key_s, bias_s, ot_s)

    y_ref[0] = ot_s[...].T


def _dsa_prompt(q3, iq3, ikw3, k3, v3):
    bsz, t, _ = q3.shape
    nqb = t // Q_BLOCK
    blk = lambda b, j: (b, j, 0)
    full = lambda b, j: (b, 0, 0)
    classes = math.gcd(nqb, KEY_CLASSES)
    assert (nqb // classes) * Q_BLOCK % KC == 0
    kern = functools.partial(_dsa_prompt_kernel, T=t, classes=classes)
    return pl.pallas_call(
        kern,
        grid=(bsz, nqb),
        in_specs=[pl.BlockSpec((1, Q_BLOCK, B_WIDTH), blk),
                  pl.BlockSpec((1, Q_BLOCK, IQ_WIDTH), blk),
                  pl.BlockSpec((1, Q_BLOCK, LANES), blk),
                  pl.BlockSpec((1, t, LANES), full),
                  pl.BlockSpec((1, t, KV_WIDTH), full),
                  pl.BlockSpec((1, t, KV_WIDTH), full)],
        out_specs=pl.BlockSpec((1, Q_BLOCK, B_WIDTH), blk),
        out_shape=jax.ShapeDtypeStruct((bsz, t, B_WIDTH), F32),
        scratch_shapes=[pltpu.VMEM((t, KV_WIDTH), BF16),
                        pltpu.VMEM((KV_WIDTH, t), BF16),
                        pltpu.VMEM((t, IDX_DIM), BF16),
                        pltpu.VMEM((t, Q_BLOCK), I32),
                        pltpu.VMEM((t, Q_BLOCK), F32),
                        pltpu.VMEM((B_WIDTH, Q_BLOCK), F32)],
        compiler_params=_cparams("parallel", "arbitrary"),
    )(q3, iq3, ikw3, ikw3, k3, v3)


PG = 16


def _sidx_kernel(pt_ref, iqf_ref, wcol_ref, *rest, Ts):
    pages = rest[:PG]
    out_ref = rest[PG]
    iqf = iqf_ref[0].astype(BF16)
    wcol = wcol_ref[0]
    for i in range(PG):
        s = _dot(iqf, pages[i][0, 0])
        r = jnp.maximum(s, 0.0) * wcol
        out_ref[0, i] = jnp.sum(r.reshape(IDX_HEADS, Ts, PAGE), axis=0)


def _sidx(page_table, iqf, wcol, cache_idx_kt, l, Ts):
    db, n_pages = page_table.shape
    steps = n_pages // PG

    def page_spec(i):
        return pl.BlockSpec((1, 1, IDX_DIM, PAGE), lambda b, j, pt: (l, pt[b, j * PG + i], 0, 0))

    grid_spec = pltpu.PrefetchScalarGridSpec(
        num_scalar_prefetch=1,
        grid=(db, steps),
        in_specs=[pl.BlockSpec((1, IDX_HEADS * Ts, IDX_DIM), lambda b, j, pt: (b, 0, 0)),
                  pl.BlockSpec((1, IDX_HEADS * Ts, 1), lambda b, j, pt: (b, 0, 0))]
                 + [page_spec(i) for i in range(PG)],
        out_specs=pl.BlockSpec((1, PG, Ts, PAGE), lambda b, j, pt: (b, j, 0, 0)),
    )
    return pl.pallas_call(
        functools.partial(_sidx_kernel, Ts=Ts),
        grid_spec=grid_spec,
        out_shape=jax.ShapeDtypeStruct((db, n_pages, Ts, PAGE), F32),
        compiler_params=_cparams("parallel", "arbitrary"),
    )(page_table, iqf, wcol, *([cache_idx_kt] * PG))


def _ssel_kernel(sc_ref, iqf_ref, wcol_ref, iknew_ref, bias_ref, key_s, *, Ts, NT, NB):
    vis = _iota((Ts, PAGE), 1) <= _iota((Ts, PAGE), 0)
    for bb in range(NB):
        key_s[bb, 0:NT] = _order_key(sc_ref[bb])
        s = _dot_nt(iqf_ref[bb], iknew_ref[bb])
        r = jnp.maximum(s, 0.0) * wcol_ref[bb]
        snew = jnp.sum(r.reshape(IDX_HEADS, Ts, PAGE), axis=0)
        key_s[bb, NT] = _order_key(jnp.where(vis, snew, NEG_INF))

    def count(mask):
        ones = jnp.where(mask, 1.0, 0.0)
        part = ones[NT]
        if NT % 8 == 0:
            part = part + jnp.sum(jnp.sum(ones[0:NT].reshape(NT // 8, 8, Ts, PAGE), axis=0), axis=0)
        else:
            part = part + jnp.sum(ones[0:NT], axis=0)
        return jnp.sum(part, axis=1, keepdims=True)

    def bit_body(i, ts):
        out = []
        for bb in range(NB):
            cand = ts[bb] + jnp.left_shift(jnp.int32(1), 31 - i)
            cnt = count(key_s[bb] >= cand[None])
            out.append(jnp.where(cnt >= TOPK, cand, ts[bb]))
        return tuple(out)

    thrs = lax.fori_loop(0, 32, bit_body, tuple(jnp.full((Ts, 1), INT_MIN, I32) for _ in range(NB)))
    for bb in range(NB):
        thr = thrs[bb]
        keys = key_s[bb]
        need = TOPK - count(keys > thr[None])
        n_eq = count(keys == thr[None])
        finite = (keys > KEY_NEG_INF) & (keys < KEY_POS_INF)
        bias_ref[bb] = jnp.where((keys >= thr[None]) & finite, 0.0, NEG_INF)
        has_tie = jnp.max(jnp.where((n_eq > need) & (thr > KEY_NEG_INF), 1.0, 0.0)) > 0.0

        @pl.when(has_tie)
        def _(bb=bb, thr=thr, need=need):
            triu = (_iota((PAGE, PAGE), 0) <= _iota((PAGE, PAGE), 1)).astype(BF16)

            def tile_body(j, run):
                kk = key_s[bb, j]
                eq = kk == thr
                eqf = jnp.where(eq, 1.0, 0.0)
                rank = jnp.dot(eqf.astype(BF16), triu, preferred_element_type=F32) + run
                sel = (kk > thr) | (eq & (rank <= need))
                ok = sel & (kk > KEY_NEG_INF) & (kk < KEY_POS_INF)
                bias_ref[bb, j] = jnp.where(ok, 0.0, NEG_INF)
                return run + jnp.sum(eqf, axis=1, keepdims=True)

            lax.fori_loop(0, NT + 1, tile_body, jnp.zeros((Ts, 1), F32))


def _ssel(scores, iqf, wcol, iknew):
    db, nt, ts, _ = scores.shape
    NB = math.gcd(db, SEQ_ROWS_PER_STEP)
    b3 = lambda b: (b, 0, 0)
    b4 = lambda b: (b, 0, 0, 0)
    return pl.pallas_call(
        functools.partial(_ssel_kernel, Ts=ts, NT=nt, NB=NB),
        grid=(db // NB,),
        in_specs=[pl.BlockSpec((NB, nt, ts, PAGE), b4),
                  pl.BlockSpec((NB, IDX_HEADS * ts, IDX_DIM), b3),
                  pl.BlockSpec((NB, IDX_HEADS * ts, 1), b3),
                  pl.BlockSpec((NB, PAGE, IDX_DIM), b3)],
        out_specs=pl.BlockSpec((NB, nt + 1, ts, PAGE), b4),
        out_shape=jax.ShapeDtypeStruct((db, nt + 1, ts, PAGE), F32),
        scratch_shapes=[pltpu.VMEM((NB, nt + 1, ts, PAGE), I32)],
        compiler_params=_cparams("parallel"),
    )(scores, iqf, wcol, iknew)


def _sattn_kernel(pt_ref, qbd_ref, bias_ref, biasn_ref, knew_ref, vnew_ref, *rest, Ts):
    kpages = rest[:PG]
    vpages = rest[PG:2 * PG]
    y_ref = rest[2 * PG]
    m_s, l_s, acc_s = rest[2 * PG + 1:]
    j = pl.program_id(1)

    @pl.when(j == 0)
    def _():
        m_s[...] = jnp.full(m_s.shape, NEG_INF, F32)
        l_s[...] = jnp.zeros(l_s.shape, F32)
        acc_s[...] = jnp.zeros(acc_s.shape, F32)

    qbd = qbd_ref[0].astype(BF16)

    def rows(b):
        return jnp.concatenate([b] * B_HEADS, axis=0)

    def update(s, pv):
        m_old = m_s[...]
        m_new = jnp.maximum(m_old, jnp.max(s, axis=1, keepdims=True))
        m_safe = jnp.where(m_new == NEG_INF, 0.0, m_new)
        scale = jnp.exp(m_old - m_safe)
        pr = jnp.exp(s - m_safe)
        l_s[...] = scale * l_s[...] + jnp.sum(pr, axis=1, keepdims=True)
        acc_s[...] = scale * acc_s[...] + pv(pr)
        m_s[...] = m_new

    s = jnp.concatenate(
        [_dot(qbd, kpages[i][0, 0].reshape(KV_WIDTH, PAGE)) + rows(bias_ref[0, i]) for i in range(PG)],
        axis=1)

    def pv(pr):
        acc = jnp.zeros((B_HEADS * Ts, KV_WIDTH), F32)
        for i in range(PG):
            acc = acc + _dot_nt(pr[:, i * PAGE:(i + 1) * PAGE], vpages[i][0, 0].reshape(KV_WIDTH, PAGE))
        return acc

    update(s, pv)

    @pl.when(j == pl.num_programs(1) - 1)
    def _():
        sn = _dot(qbd, knew_ref[0]) + rows(biasn_ref[0, 0])
        update(sn, lambda pr: _dot_nt(pr, vnew_ref[0]))
        o = acc_s[...] / l_s[...]
        for h in range(B_HEADS):
            g = h // (B_HEADS // B_KV_HEADS)
            y_ref[0, :, h * HEAD_DIM:(h + 1) * HEAD_DIM] = o[h * Ts:(h + 1) * Ts, g * HEAD_DIM:(g + 1) * HEAD_DIM]


def _sattn(page_table, qbd, bias, knew_t, vnew_t, cache_kt, cache_vt, l, Ts):
    db, n_pages = page_table.shape
    steps = n_pages // PG

    def page_spec(i):
        return pl.BlockSpec((1, 1, B_KV_HEADS, HEAD_DIM, PAGE),
                            lambda b, j, pt: (l, pt[b, j * PG + i], 0, 0, 0))

    b3 = lambda b, j, pt: (b, 0, 0)
    nrows = B_HEADS * Ts
    grid_spec = pltpu.PrefetchScalarGridSpec(
        num_scalar_prefetch=1,
        grid=(db, steps),
        in_specs=[pl.BlockSpec((1, nrows, KV_WIDTH), b3),
                  pl.BlockSpec((1, PG, Ts, PAGE), lambda b, j, pt: (b, j, 0, 0)),
                  pl.BlockSpec((1, 1, Ts, PAGE), lambda b, j, pt: (b, n_pages, 0, 0)),
                  pl.BlockSpec((1, KV_WIDTH, PAGE), b3),
                  pl.BlockSpec((1, KV_WIDTH, PAGE), b3)]
                 + [page_spec(i) for i in range(PG)] + [page_spec(i) for i in range(PG)],
        out_specs=pl.BlockSpec((1, Ts, B_WIDTH), b3),
        scratch_shapes=[pltpu.VMEM((nrows, 1), F32),
                        pltpu.VMEM((nrows, 1), F32),
                        pltpu.VMEM((nrows, KV_WIDTH), F32)],
    )
    return pl.pallas_call(
        functools.partial(_sattn_kernel, Ts=Ts),
        grid_spec=grid_spec,
        out_shape=jax.ShapeDtypeStruct((db, Ts, B_WIDTH), F32),
        compiler_params=_cparams("parallel", "arbitrary"),
    )(page_table, qbd, bias, bias, knew_t, vnew_t, *([cache_kt] * PG), *([cache_vt] * PG))


def _dsa_sample(q3, iq3, ikw3, k3, v3, cache_kt, cache_vt, cache_idx_kt, page_table, l):
    db, ts, _ = q3.shape
    iqf = iq3.reshape(db, ts, IDX_HEADS, IDX_DIM).transpose(0, 2, 1, 3).reshape(db, IDX_HEADS * ts, IDX_DIM)
    iw = ikw3[:, :, IDX_DIM:IDX_DIM + IDX_HEADS] * (IDX_DIM ** -0.5 * IDX_HEADS ** -0.5)
    wcol = iw.transpose(0, 2, 1).reshape(db, IDX_HEADS * ts, 1)
    qh = (q3 * (HEAD_DIM ** -0.5)).reshape(db, ts, B_HEADS, HEAD_DIM).transpose(0, 2, 1, 3)
    own = (jnp.arange(B_HEADS)[:, None] // (B_HEADS // B_KV_HEADS)) == jnp.arange(B_KV_HEADS)[None, :]
    qbd = jnp.where(own[None, :, None, :, None], qh[:, :, :, None, :], 0.0)
    qbd = qbd.reshape(db, B_HEADS * ts, KV_WIDTH)
    pad = lambda a: jnp.pad(a, ((0, 0), (0, PAGE - ts), (0, 0)))
    iknew = pad(ikw3[:, :, :IDX_DIM])
    scores = _sidx(page_table, iqf, wcol, cache_idx_kt, l, ts)
    bias = _ssel(scores, iqf, wcol, iknew)
    knew_t = pad(k3).transpose(0, 2, 1)
    vnew_t = pad(v3).transpose(0, 2, 1)
    return _sattn(page_table, qbd, bias, knew_t, vnew_t, cache_kt, cache_vt, l, ts)


def _ffn_kernel(x_ref, ya_ref, yb_ref, yc_ref, wo_ref, gf_ref, wg_ref, wu_ref, wd_ref, gfin_ref,
                o_ref, *, final):
    mixed = (jnp.dot(ya_ref[...].astype(BF16), wo_ref[0:A_WIDTH, :], preferred_element_type=F32)
             + jnp.dot(yb_ref[...].astype(BF16), wo_ref[A_WIDTH:A_WIDTH + B_WIDTH, :],
                       preferred_element_type=F32)
             + jnp.dot(yc_ref[...].astype(BF16), wo_ref[A_WIDTH + B_WIDTH:, :],
                       preferred_element_type=F32))
    x1 = x_ref[...] + mixed
    ms = jnp.mean(x1 * x1, axis=-1, keepdims=True)
    hf = (x1 * lax.rsqrt(ms + NORM_EPS) * gf_ref[...]).astype(BF16)
    gate = jnp.dot(hf, wg_ref[...], preferred_element_type=F32)
    up = jnp.dot(hf, wu_ref[...], preferred_element_type=F32)
    act = (gate * _sigmoid(gate) * up).astype(BF16)
    x2 = x1 + jnp.dot(act, wd_ref[...], preferred_element_type=F32)
    if final:
        ms2 = jnp.mean(x2 * x2, axis=-1, keepdims=True)
        x2 = x2 * lax.rsqrt(ms2 + NORM_EPS) * gfin_ref[...]
    o_ref[...] = x2


def _ffn(x2d, ya, yb, yc, wo, gf, wg, wu, wd, gfin, final, tm):
    n, d = x2d.shape
    dff = wg.shape[1]
    row = lambda i: (i, 0)
    fixed = lambda i: (0, 0)
    once = dict(pipeline_mode=pl.Buffered(1))
    return pl.pallas_call(
        functools.partial(_ffn_kernel, final=final),
        grid=(n // tm,),
        in_specs=[pl.BlockSpec((tm, d), row), pl.BlockSpec((tm, A_WIDTH), row),
                  pl.BlockSpec((tm, B_WIDTH), row), pl.BlockSpec((tm, C_WIDTH), row),
                  pl.BlockSpec((d, d), fixed, **once), pl.BlockSpec((1, d), fixed),
                  pl.BlockSpec((d, dff), fixed, **once), pl.BlockSpec((d, dff), fixed, **once),
                  pl.BlockSpec((dff, d), fixed, **once), pl.BlockSpec((1, d), fixed)],
        out_specs=pl.BlockSpec((tm, d), row),
        out_shape=jax.ShapeDtypeStruct((n, d), F32),
        compiler_params=_cparams("parallel"),
    )(x2d, ya, yb, yc, wo, gf, wg, wu, wd, gfin)


def _rope_tables(pos):
    half = ROT_DIM // 2
    inv = ROPE_THETA ** (-jnp.arange(half, dtype=F32) / half)
    ang = pos.astype(F32)[:, None] * inv[None, :]
    j = jnp.arange(LANES) % HEAD_DIM
    cos = jnp.cos(ang)[:, j % half]
    sin = jnp.sin(ang)[:, j % half]
    cos_t = jnp.where(j[None, :] < ROT_DIM, cos, 1.0)
    sina = jnp.where(j[None, :] < half, -sin, 0.0)
    sinb = jnp.where((j[None, :] >= half) & (j[None, :] < ROT_DIM), sin, 0.0)
    return cos_t, sina, sinb


def _pad_w_in(w):
    d = w.shape[0]
    z = lambda n: jnp.zeros((d, n), w.dtype)
    gates = jnp.repeat(w[:, 4 * A_WIDTH:A_PROJ], HEAD_DIM, axis=1)
    return jnp.concatenate([w[:, :4 * A_WIDTH], gates,
                            w[:, A_PROJ:A_PROJ + B_PROJ], z(B_PAD - B_PROJ),
                            w[:, A_PROJ + B_PROJ:]], axis=1).astype(BF16)


def _layer(x3, l, tabs, mstate, shift0, s0, attn_fn, p, wts, final, tm):
    bsz, t, d = x3.shape
    n = bsz * t
    pa, q, k, v, iq, ikw, pc = _inproj(x3.reshape(n, d), p['norm_mix'][l].reshape(1, d), wts['w_in'][l], tabs, tm)
    r3 = lambda a: a.reshape(bsz, t, a.shape[-1])
    y_a, c_new, n_new, m_new = _mlstm(r3(pa), *mstate, p['mlstm_gate_b'][l], p['mlstm_norm'][l])
    y_b = attn_fn(r3(q), r3(iq), r3(ikw), r3(k), r3(v))
    y_c, shift, s_new = _rwkv(r3(pc), shift0, s0, p, l)
    x_new = _ffn(x3.reshape(n, d), y_a.reshape(n, -1), y_b.reshape(n, -1), y_c.reshape(n, -1),
                 wts['w_out'][l], p['norm_ffn'][l].reshape(1, d), wts['w_gate'][l], wts['w_up'][l],
                 wts['w_down'][l], p['norm_final'].reshape(1, d), final, tm)
    new = (k.reshape(bsz, t, B_KV_HEADS, HEAD_DIM), v.reshape(bsz, t, B_KV_HEADS, HEAD_DIM),
           r3(ikw)[:, :, :IDX_DIM], c_new, n_new, m_new, shift, s_new)
    return x_new.reshape(bsz, t, d), new


def kernel(x_prompt, x_sample, cache_k, cache_v, cache_idx_k, state_mlstm_C, state_mlstm_n, state_mlstm_m, state_rwkv_shift, state_rwkv_S, page_table, norm_mix, w_in, mlstm_gate_b, mlstm_norm, rwkv_mu, rwkv_w0, rwkv_w2, rwkv_a0, rwkv_a2, rwkv_g2, rwkv_k_k, rwkv_k_a, rwkv_r_k, rwkv_ln_w, rwkv_ln_b, w_out, norm_ffn, w_gate, w_up, w_down, norm_final):
    p = dict(norm_mix=norm_mix, mlstm_gate_b=mlstm_gate_b, mlstm_norm=mlstm_norm,
             rwkv_mu=rwkv_mu, rwkv_w0=rwkv_w0, rwkv_w2=rwkv_w2, rwkv_a0=rwkv_a0, rwkv_a2=rwkv_a2,
             rwkv_g2=rwkv_g2, rwkv_k_k=rwkv_k_k, rwkv_k_a=rwkv_k_a, rwkv_r_k=rwkv_r_k,
             rwkv_ln_w=rwkv_ln_w, rwkv_ln_b=rwkv_ln_b, norm_ffn=norm_ffn, norm_final=norm_final)
    depth = w_in.shape[0]
    wts = dict(w_in=[_pad_w_in(w_in[l]) for l in range(depth)],
               w_out=w_out.astype(BF16), w_gate=w_gate.astype(BF16),
               w_up=w_up.astype(BF16), w_down=w_down.astype(BF16))
    bp, tp, _ = x_prompt.shape
    db, ts, _ = x_sample.shape
    past = page_table.shape[1] * PAGE
    tabs_p = _rope_tables(jnp.arange(tp, dtype=I32))
    tabs_s = tuple(jnp.tile(a, (db, 1)) for a in _rope_tables(past + jnp.arange(ts, dtype=I32)))
    tm_p = min(256, bp * tp)
    tm_s = db * ts
    zc = jnp.zeros((bp, A_HEADS, HEAD_DIM, HEAD_DIM), F32)
    zn = jnp.zeros((bp, A_HEADS, HEAD_DIM), F32)
    zm = jnp.zeros((bp, A_HEADS), F32)
    zshift = jnp.zeros((bp, C_PROJ), F32)
    zs = jnp.zeros((bp, C_HEADS, HEAD_DIM, HEAD_DIM), F32)
    cache_kt = cache_k.transpose(0, 1, 3, 4, 2)
    cache_vt = cache_v.transpose(0, 1, 3, 4, 2)
    cache_idx_kt = cache_idx_k.transpose(0, 1, 3, 2)
    xp, xs = x_prompt, x_sample
    new_p, new_s = [], []
    for l in range(depth):
        final = l == depth - 1
        xp, st = _layer(xp, l, tabs_p, (zc, zn, zm), zshift, zs, _dsa_prompt, p, wts, final, tm_p)
        new_p.append(st)
        attn_s = functools.partial(_dsa_sample, cache_kt=cache_kt, cache_vt=cache_vt,
                                   cache_idx_kt=cache_idx_kt, page_table=page_table, l=l)
        xs, st = _layer(xs, l, tabs_s, (state_mlstm_C[l], state_mlstm_n[l], state_mlstm_m[l]),
                        state_rwkv_shift[l], state_rwkv_S[l], attn_s, p, wts, final, tm_s)
        new_s.append(st)
    stack = lambda states, i: jnp.stack([st[i] for st in states])
    outs_p = [stack(new_p, i) for i in range(8)]
    outs_s = [stack(new_s, i) for i in range(8)]
    return (xp, xs, *outs_p, *outs_s)
```

```python
import functools
import math

import jax
import jax.numpy as jnp
from jax import lax
from jax.experimental import pallas as pl
from jax.experimental.pallas import tpu as pltpu

F32 = jnp.float32
BF16 = jnp.bfloat16
I32 = jnp.int32

HEAD_DIM = 64
A_HEADS = 4
B_HEADS = 8
B_KV_HEADS = 4
C_HEADS = 4
IDX_HEADS = 8
IDX_DIM = 64
TOPK = 256
Q_BLOCK = 128
ROT_DIM = 16
ROPE_THETA = 500000.0
MLSTM_CHUNK = 64
RWKV_CHUNK = 64
SEQ_ROWS_PER_STEP = 4
RWKV_LN_EPS = 64e-5
NORM_EPS = 1e-6
PAGE = 128

A_WIDTH = A_HEADS * HEAD_DIM
B_WIDTH = B_HEADS * HEAD_DIM
KV_WIDTH = B_KV_HEADS * HEAD_DIM
C_WIDTH = C_HEADS * HEAD_DIM
IQ_WIDTH = IDX_HEADS * IDX_DIM
A_PROJ = 4 * A_WIDTH + 2 * A_HEADS
B_PROJ = B_WIDTH + 2 * KV_WIDTH + IQ_WIDTH + IDX_DIM + IDX_HEADS
C_PROJ = 3 * C_WIDTH + 64 + 64 + 128

LANES = 128
A_PAD = 6 * A_WIDTH
B_PAD = B_PROJ + (LANES - (IDX_DIM + IDX_HEADS))
P_PAD = A_PAD + B_PAD + C_PROJ

VMEM_LIMIT = 56 * 1024 * 1024

INT_MIN = -2147483648
KEY_NEG_INF = -2139095040
KEY_POS_INF = 2139095040
NEG_INF = float("-inf")
PAD_LOG_GATE = -1e30
SUBLANES = 8
BF16_ROWS = 16


def _cparams(*sem):
    return pltpu.CompilerParams(dimension_semantics=sem, vmem_limit_bytes=VMEM_LIMIT)


def _dot(a, b):
    return jnp.dot(a.astype(BF16), b.astype(BF16), preferred_element_type=F32)


def _dot_nt(a, b):
    return lax.dot_general(a.astype(BF16), b.astype(BF16), (((1,), (1,)), ((), ())),
                           preferred_element_type=F32)


def _split3(a):
    hi = a.astype(BF16)
    r = a - hi.astype(F32)
    mid = r.astype(BF16)
    lo = (r - mid.astype(F32)).astype(BF16)
    return hi, mid, lo


_NN = (((1,), (0,)), ((), ()))
_NT = (((1,), (1,)), ((), ()))
_TN = (((0,), (0,)), ((), ()))


def _dg(a, b, dims):
    return lax.dot_general(a, b, dims, preferred_element_type=F32)


def _dot01_left(m01, x):
    m = m01.astype(BF16)
    x1, x2, x3 = _split3(x)
    return _dg(m, x1, _NN) + (_dg(m, x2, _NN) + _dg(m, x3, _NN))


def _dot01_right(x, m01):
    m = m01.astype(BF16)
    x1, x2, x3 = _split3(x)
    return _dg(x1, m, _NN) + (_dg(x2, m, _NN) + _dg(x3, m, _NN))


def _sigmoid(x):
    return 1.0 / (1.0 + jnp.exp(-x))


def _softplus(x):
    return jnp.maximum(x, 0.0) + jnp.log1p(jnp.exp(-jnp.abs(x)))


def _iota(shape, dim):
    return lax.broadcasted_iota(I32, shape, dim)


def _round_robin(gens):
    gens = list(gens)
    while gens:
        alive = []
        for g in gens:
            try:
                next(g)
                alive.append(g)
            except StopIteration:
                pass
        gens = alive


def _reduce_rows(x, op):
    n, w = x.shape
    group = 8 * SUBLANES
    if n > group and n % group == 0:
        x = op(x.reshape(n // group, group, w), axis=0)
    return op(x, axis=0, keepdims=True)


def _rope_tile(xt, cos, sina, sinb):
    up = pltpu.roll(xt, LANES - ROT_DIM // 2, axis=1)
    dn = pltpu.roll(xt, ROT_DIM // 2, axis=1)
    return xt * cos + up * sina + dn * sinb


def _inproj_kernel(x_ref, g_ref, w_ref, cos_ref, sina_ref, sinb_ref,
                   pa_ref, q_ref, k_ref, v_ref, iq_ref, ikw_ref, pc_ref):
    x = x_ref[...]
    ms = jnp.mean(x * x, axis=-1, keepdims=True)
    h = (x * lax.rsqrt(ms + NORM_EPS) * g_ref[...]).astype(BF16)
    cos = cos_ref[...]
    sina = sina_ref[...]
    sinb = sinb_ref[...]

    def proj(lo, width):
        return jnp.dot(h, w_ref[:, lo:lo + width], preferred_element_type=F32)

    def rope(x2):
        tiles = [_rope_tile(x2[:, j * LANES:(j + 1) * LANES], cos, sina, sinb)
                 for j in range(x2.shape[1] // LANES)]
        return tiles[0] if len(tiles) == 1 else jnp.concatenate(tiles, axis=1)

    pa_ref[...] = proj(0, A_PAD)
    b0 = A_PAD
    q_ref[...] = rope(proj(b0, B_WIDTH))
    k_ref[...] = rope(proj(b0 + B_WIDTH, KV_WIDTH))
    v_ref[...] = proj(b0 + B_WIDTH + KV_WIDTH, KV_WIDTH)
    iq_ref[...] = rope(proj(b0 + B_WIDTH + 2 * KV_WIDTH, IQ_WIDTH))
    ikw = proj(b0 + B_WIDTH + 2 * KV_WIDTH + IQ_WIDTH, LANES)
    lane = _iota(ikw.shape, 1)
    ikw_ref[...] = jnp.where(lane < IDX_DIM, _rope_tile(ikw, cos, sina, sinb), ikw)
    pc_ref[...] = proj(A_PAD + B_PAD, C_PROJ)


def _inproj(x2d, g, w_bf, tabs, tm):
    n, d = x2d.shape
    cos, sina, sinb = tabs
    tab_blocks = cos.shape[0] // tm
    row = lambda i: (i, 0)
    fixed = lambda i: (0, 0)
    tab = lambda i: (i % tab_blocks, 0)
    widths = (A_PAD, B_WIDTH, KV_WIDTH, KV_WIDTH, IQ_WIDTH, LANES, C_PROJ)
    return pl.pallas_call(
        _inproj_kernel,
        grid=(n // tm,),
        in_specs=[pl.BlockSpec((tm, d), row), pl.BlockSpec((1, d), fixed),
                  pl.BlockSpec((d, P_PAD), fixed),
                  pl.BlockSpec((tm, LANES), tab), pl.BlockSpec((tm, LANES), tab),
                  pl.BlockSpec((tm, LANES), tab)],
        out_specs=[pl.BlockSpec((tm, w), row) for w in widths],
        out_shape=[jax.ShapeDtypeStruct((n, w), F32) for w in widths],
        compiler_params=_cparams("parallel"),
    )(x2d, g, w_bf, cos, sina, sinb)


def _mlstm_kernel(pa_ref, c0_ref, n0_ref, m0_ref, gb_ref, gn_ref,
                  y_ref, cout_ref, nout_ref, mout_ref, c_s, n_s, m_s, *, L, Lp, NB):
    c = pl.program_id(1)
    W = A_WIDTH
    HL = A_HEADS * Lp

    @pl.when(c == 0)
    def _():
        n_s[...] = n0_ref[...]
        m_s[...] = m0_ref[...]
        c_s[...] = jnp.zeros((NB, W, W), F32)
        for bb in range(NB):
            for h in range(A_HEADS):
                sl = slice(h * HEAD_DIM, (h + 1) * HEAD_DIM)
                c_s[bb, sl, sl] = c0_ref[bb, h]

    rowi = _iota((Lp, W), 0)
    valid = rowi < L
    tri = _iota((Lp, Lp), 0) >= _iota((Lp, Lp), 1)
    ones_ll = jnp.ones((Lp, Lp), F32)
    key_of_lane = _iota((Lp, HL), 1) % Lp
    eye_t = _iota((Lp, HL), 0) == key_of_lane
    causal_t = _iota((Lp, HL), 0) >= key_of_lane
    lane_h = _iota((Lp, W), 1) // HEAD_DIM
    same_head = _iota((W, W), 0) // HEAD_DIM == _iota((W, W), 1) // HEAD_DIM

    def stack(x):
        return jnp.concatenate([jnp.where(lane_h == h, x, 0.0) for h in range(A_HEADS)], axis=0)

    def head_sum(x):
        return _dot01_right(x, same_head)

    def row(bb):
        pa = pa_ref[bb]
        if Lp > L:
            pa = jnp.concatenate([pa, jnp.zeros((Lp - L, pa.shape[1]), F32)], axis=0)
        q = pa[:, 0:W]
        k = pa[:, W:2 * W] * (HEAD_DIM ** -0.5)
        v = pa[:, 2 * W:3 * W]
        o = pa[:, 3 * W:4 * W]
        gi = pa[:, 4 * W:5 * W] + gb_ref[:, 0:W]
        gf = pa[:, 5 * W:6 * W] + gb_ref[:, W:2 * W]
        log_i = jnp.where(valid, gi, PAD_LOG_GATE)
        log_f = jnp.where(valid, -_softplus(-gf), 0.0)
        k_st = stack(k).astype(BF16)
        v_st = stack(v).astype(BF16)
        q_bf = q.astype(BF16)
        yield
        b = _dot01_left(tri, log_f)
        qk = _dg(q_bf, k_st, _NT)
        c_bd = c_s[bb]
        n_row = n_s[bb]
        qc = _dg(q_bf, c_bd.astype(BF16), _NT)
        qn = head_sum(q * n_row)
        yield
        ib = log_i - b
        cm = ib
        sh = 1
        while sh < Lp:
            cm = jnp.where(rowi >= sh, jnp.maximum(cm, pltpu.roll(cm, sh, axis=0)), cm)
            sh *= 2
        m_prev = m_s[bb]
        mx = jnp.maximum(m_prev, cm)
        m_t = b + mx
        w_inter = jnp.exp(m_prev - mx)
        ib_diag = jnp.where(eye_t, ib, 0.0)
        mx_last = mx[Lp - 1:Lp, :]
        wk = jnp.exp(ib - mx_last)
        dec = jnp.exp(m_prev - mx_last)
        vw = (v * wk).astype(BF16)
        yield
        ib_row = _dot01_left(ones_ll, ib_diag)
        c_upd = _dg(vw, k.astype(BF16), _TN)
        yield
        c_s[bb] = dec * c_bd + jnp.where(same_head, c_upd, 0.0)
        n_s[bb] = dec * n_row + jnp.sum(k * wk, axis=0, keepdims=True)
        m_s[bb] = b[Lp - 1:Lp, :] + mx_last
        pmat = jnp.where(causal_t, jnp.exp(ib_row - mx), 0.0)
        s_all = qk * pmat
        s_bf = s_all.astype(BF16)
        yield
        num = _dg(s_bf, v_st, _NN) + w_inter * qc
        den = head_sum(s_all) + w_inter * qn
        yield
        hh = num / jnp.maximum(jnp.abs(den), jnp.exp(-m_t))
        hsq = hh * hh
        yield
        ms = head_sum(hsq) * (1.0 / HEAD_DIM)
        yield
        hn = hh * lax.rsqrt(ms + NORM_EPS) * gn_ref[...]
        y_ref[bb] = (_sigmoid(o) * hn)[:L]

    _round_robin([row(bb) for bb in range(NB)])

    @pl.when(c == pl.num_programs(1) - 1)
    def _():
        nout_ref[...] = n_s[...]
        mout_ref[...] = m_s[...]
        for bb in range(NB):
            for h in range(A_HEADS):
                sl = slice(h * HEAD_DIM, (h + 1) * HEAD_DIM)
                cout_ref[bb, h] = c_s[bb, sl, sl]


def _mlstm(pa3, c0, n0, m0, gate_b, gnorm):
    bsz, t, _ = pa3.shape
    L = min(t, MLSTM_CHUNK)
    Lp = MLSTM_CHUNK
    nc = t // L
    assert Lp == HEAD_DIM
    NB = math.gcd(bsz, SEQ_ROWS_PER_STEP)
    gb = jnp.repeat(gate_b, HEAD_DIM).reshape(1, 2 * A_WIDTH)
    m0_rep = jnp.repeat(m0, HEAD_DIM, axis=1).reshape(bsz, 1, A_WIDTH)
    kern = functools.partial(_mlstm_kernel, L=L, Lp=Lp, NB=NB)
    st4 = lambda b, c: (b, 0, 0, 0)
    st3 = lambda b, c: (b, 0, 0)
    fixed = lambda b, c: (0, 0)
    y, cn, nn, mn = pl.pallas_call(
        kern,
        grid=(bsz // NB, nc),
        in_specs=[pl.BlockSpec((NB, L, A_PAD), lambda b, c: (b, c, 0)),
                  pl.BlockSpec((NB, A_HEADS, HEAD_DIM, HEAD_DIM), st4),
                  pl.BlockSpec((NB, 1, A_WIDTH), st3),
                  pl.BlockSpec((NB, 1, A_WIDTH), st3),
                  pl.BlockSpec((1, 2 * A_WIDTH), fixed),
                  pl.BlockSpec((1, A_WIDTH), fixed)],
        out_specs=[pl.BlockSpec((NB, L, A_WIDTH), lambda b, c: (b, c, 0)),
                   pl.BlockSpec((NB, A_HEADS, HEAD_DIM, HEAD_DIM), st4),
                   pl.BlockSpec((NB, 1, A_WIDTH), st3),
                   pl.BlockSpec((NB, 1, A_WIDTH), st3)],
        out_shape=[jax.ShapeDtypeStruct((bsz, t, A_WIDTH), F32),
                   jax.ShapeDtypeStruct((bsz, A_HEADS, HEAD_DIM, HEAD_DIM), F32),
                   jax.ShapeDtypeStruct((bsz, 1, A_WIDTH), F32),
                   jax.ShapeDtypeStruct((bsz, 1, A_WIDTH), F32)],
        scratch_shapes=[pltpu.VMEM((NB, A_WIDTH, A_WIDTH), F32),
                        pltpu.VMEM((NB, 1, A_WIDTH), F32),
                        pltpu.VMEM((NB, 1, A_WIDTH), F32)],
        compiler_params=_cparams("parallel", "arbitrary"),
    )(pa3, c0, n0.reshape(bsz, 1, A_WIDTH), m0_rep, gb, gnorm.reshape(1, A_WIDTH))
    return y, cn, nn.reshape(bsz, A_HEADS, HEAD_DIM), mn[:, 0, ::HEAD_DIM]


def _rwkv_kernel(pc_ref, shift0_ref, s0_ref, mu_ref, w0_ref, w2_ref, a0_ref, a2_ref, g2_ref,
                 kk_ref, ka_ref, rk_ref, lnw_ref, lnb_ref,
                 y_ref, shift_out_ref, s_out_ref, s_s, carry_s, *, L, C, NB):
    c = pl.program_id(1)
    last = c == pl.num_programs(1) - 1
    W = C_WIDTH
    HC = C_HEADS * C

    @pl.when(c == 0)
    def _():
        carry_s[...] = shift0_ref[...]
        s_s[...] = jnp.zeros((NB, W, W), F32)
        for bb in range(NB):
            for h in range(C_HEADS):
                sl = slice(h * HEAD_DIM, (h + 1) * HEAD_DIM)
                s_s[bb, sl, sl] = s0_ref[bb, h]

    rowc = _iota((C, C_PROJ), 0)
    lane_head = _iota((W, W), 0) // HEAD_DIM == _iota((W, W), 1) // HEAD_DIM
    tri = _iota((C, C), 0) >= _iota((C, C), 1)
    lane_h = _iota((C, W), 1) // HEAD_DIM
    vrow = _iota((C, W), 0) < L
    tt = _iota((HC, HC), 0) % C
    ii = _iota((HC, HC), 1) % C
    eye = (_iota((HC, HC), 0) == _iota((HC, HC), 1)).astype(F32)

    def head_sum(x):
        return _dot01_right(x, lane_head)

    def stack(x):
        return jnp.concatenate([jnp.where(lane_h == h, x, 0.0) for h in range(C_HEADS)], axis=0)

    def level_mask(s):
        return ((tt // s) % 2 == 1) & ((ii // s) % 2 == 0) & (tt // (2 * s) == ii // (2 * s))

    levels = []
    s = 1
    while s < C:
        levels.append(level_mask(s))
        s *= 2

    def row(bb):
        pc = pc_ref[bb]
        if C > L:
            pc = jnp.concatenate([pc, jnp.zeros((C - L, pc.shape[1]), F32)], axis=0)
        prev = jnp.where(rowc == 0, carry_s[bb], pltpu.roll(pc, 1, axis=0))
        carry_s[bb] = pc[L - 1:L, :]
        pcs = pc + mu_ref[...] * (prev - pc)
        cr = pcs[:, 0:W]
        ck = pcs[:, W:2 * W]
        cv = pcs[:, 2 * W:3 * W]
        cwl = pcs[:, 3 * W:3 * W + 64]
        cal = pcs[:, 3 * W + 64:3 * W + 128]
        cgl = pcs[:, 3 * W + 128:3 * W + 256]
        wlog = -_softplus(-(w0_ref[...] + _dot(jnp.tanh(cwl), w2_ref[...]))) - 0.5
        lw = -jnp.exp(wlog)
        a = _sigmoid(a0_ref[...] + _dot(cal, a2_ref[...]))
        g = _dot(_sigmoid(cgl), g2_ref[...])
        kk = ck * kk_ref[...]
        kkn = kk / jnp.maximum(jnp.sqrt(head_sum(kk * kk)), 1e-12)
        k2 = ck * (1.0 + (a - 1.0) * ka_ref[...])
        alpha = -kkn
        beta = kkn * a
        if C > L:
            zero = lambda z: jnp.where(vrow, z, 0.0)
            lw, alpha, beta, k2s, cvs, crs = zero(lw), zero(alpha), zero(beta), zero(k2), zero(cv), zero(cr)
        else:
            k2s, cvs, crs = k2, cv, cr

        logp = _dot01_left(tri, lw)
        p = jnp.exp(logp)
        pinv = jnp.exp(-logp)
        pprev = jnp.exp(logp - lw)
        p_last = p[C - 1:C, :]
        a_st, r_st = stack(alpha * pprev), stack(crs * p)
        b_st, k_st, v_st = stack(beta * pinv), stack(k2s * pinv), stack(cvs)
        a_bf, b_bf, k_bf, v_bf = a_st.astype(BF16), b_st.astype(BF16), k_st.astype(BF16), v_st.astype(BF16)
        ar = jnp.concatenate([a_bf, r_st.astype(BF16)], axis=0)
        yield
        gb = _dg(ar, b_bf, _NT)
        gk = _dg(ar, k_bf, _NT)
        kv = _dg(v_bf, k_bf, _TN)
        yield
        l_ab = jnp.where(tt > ii, gb[:HC], 0.0)
        l_rb = jnp.where(tt >= ii, gb[HC:], 0.0).astype(BF16)
        l_ak = jnp.where(tt > ii, gk[:HC], 0.0).astype(BF16)
        l_rk = jnp.where(tt >= ii, gk[HC:], 0.0).astype(BF16)
        lakv = _dg(l_ak, v_bf, _NN).astype(BF16)
        lrkv = _dg(l_rk, v_bf, _NN)

        x = eye + jnp.where(levels[0], l_ab, 0.0)
        for lvl in levels[1:]:
            e = jnp.where(lvl, l_ab, 0.0).astype(BF16)
            xb = x.astype(BF16)
            yield
            xe = _dg(xb, e, _NN).astype(BF16)
            yield
            x = x + _dg(xe, xb, _NN)

        xb = x.astype(BF16)
        yield
        a_t = _dg(xb, a_bf, _NN)
        v_t = _dg(xb, lakv, _NN)
        yield
        s_old = s_s[bb]
        s_bf = s_old.astype(BF16)
        a_t_bf = a_t.astype(BF16)
        u = _dg(a_t_bf, s_bf, _NT) + v_t
        r_q = (r_st + _dg(l_rb, a_t_bf, _NN)).astype(BF16)
        y0 = _dg(l_rb, v_t.astype(BF16), _NN) + lrkv
        yield
        y_st = _dg(r_q, s_bf, _NT) + y0
        s_new = (s_old + _dg(u.astype(BF16), b_bf, _TN) + kv) * p_last
        s_s[bb] = s_new
        yield
        yc = y_st[0:C]
        for h in range(1, C_HEADS):
            yc = yc + y_st[h * C:(h + 1) * C]

        inv_d = 1.0 / HEAD_DIM
        mean = head_sum(yc) * inv_d
        dlt = yc - mean
        var = head_sum(dlt * dlt) * inv_d
        ycn = dlt * lax.rsqrt(var + RWKV_LN_EPS) * lnw_ref[...] + lnb_ref[...]
        bonus = head_sum(cr * k2 * rk_ref[...]) * cv
        y_ref[bb] = ((ycn + bonus) * g)[:L]

    _round_robin([row(bb) for bb in range(NB)])

    @pl.when(last)
    def _():
        shift_out_ref[...] = carry_s[...]
        for bb in range(NB):
            for h in range(C_HEADS):
                sl = slice(h * HEAD_DIM, (h + 1) * HEAD_DIM)
                s_out_ref[bb, h] = s_s[bb, sl, sl]


def _rwkv(pc3, shift0, s0, p, l):
    bsz, t, _ = pc3.shape
    C = RWKV_CHUNK
    L = min(t, C)
    nc = t // L
    NB = math.gcd(bsz, SEQ_ROWS_PER_STEP)
    r1 = lambda a: a.reshape(1, -1)
    params = [r1(p['rwkv_mu'][l]), r1(p['rwkv_w0'][l]), p['rwkv_w2'][l], r1(p['rwkv_a0'][l]),
              p['rwkv_a2'][l], p['rwkv_g2'][l], r1(p['rwkv_k_k'][l]), r1(p['rwkv_k_a'][l]),
              r1(p['rwkv_r_k'][l]), r1(p['rwkv_ln_w'][l]), r1(p['rwkv_ln_b'][l])]
    fixed = lambda b, c: (0, 0)
    pspecs = [pl.BlockSpec(a.shape, fixed) for a in params]
    kern = functools.partial(_rwkv_kernel, L=L, C=C, NB=NB)
    y, shift, s_new = pl.pallas_call(
        kern,
        grid=(bsz // NB, nc),
        in_specs=[pl.BlockSpec((NB, L, C_PROJ), lambda b, c: (b, c, 0)),
                  pl.BlockSpec((NB, 1, C_PROJ), lambda b, c: (b, 0, 0)),
                  pl.BlockSpec((NB, C_HEADS, HEAD_DIM, HEAD_DIM), lambda b, c: (b, 0, 0, 0))] + pspecs,
        out_specs=[pl.BlockSpec((NB, L, C_WIDTH), lambda b, c: (b, c, 0)),
                   pl.BlockSpec((NB, 1, C_PROJ), lambda b, c: (b, 0, 0)),
                   pl.BlockSpec((NB, C_HEADS, HEAD_DIM, HEAD_DIM), lambda b, c: (b, 0, 0, 0))],
        out_shape=[jax.ShapeDtypeStruct((bsz, t, C_WIDTH), F32),
                   jax.ShapeDtypeStruct((bsz, 1, C_PROJ), F32),
                   jax.ShapeDtypeStruct((bsz, C_HEADS, HEAD_DIM, HEAD_DIM), F32)],
        scratch_shapes=[pltpu.VMEM((NB, C_WIDTH, C_WIDTH), F32), pltpu.VMEM((NB, 1, C_PROJ), F32)],
        compiler_params=_cparams("parallel", "arbitrary"),
    )(pc3, shift0.reshape(bsz, 1, C_PROJ), s0, *params)
    return y, shift.reshape(bsz, C_PROJ), s_new


def _order_key(score):
    bits = pltpu.bitcast(score, I32)
    return jnp.where(bits < 0, (bits ^ 0x7FFFFFFF) + 1, bits)


KC = 256
KEY_CLASSES = 4


def _dsa_prompt_block(nk, qb, iq_t, w_t, q_t, kbf_s, vt_s, ikbf_s, key_s, bias_s, ot_s):
    ik = ikbf_s[0:nk, :]
    acc = jnp.zeros((nk, Q_BLOCK), F32)
    for hp in range(IDX_HEADS // 2):
        h0, h1 = 2 * hp, 2 * hp + 1
        iq_pair = jnp.concatenate([iq_t[h0 * IDX_DIM:(h0 + 1) * IDX_DIM, :],
                                   iq_t[h1 * IDX_DIM:(h1 + 1) * IDX_DIM, :]], axis=1)
        st = jnp.dot(ik, iq_pair, preferred_element_type=F32)
        acc = acc + jnp.maximum(st[:, :Q_BLOCK], 0.0) * w_t[h0:h0 + 1, :]
        acc = acc + jnp.maximum(st[:, Q_BLOCK:], 0.0) * w_t[h1:h1 + 1, :]
    visible = _iota((nk, Q_BLOCK), 0) <= qb * Q_BLOCK + _iota((nk, Q_BLOCK), 1)
    key_s[0:nk, :] = _order_key(jnp.where(visible, acc, NEG_INF))

    def count(pred):
        return _reduce_rows(jnp.where(pred(key_s[0:nk, :]), 1.0, 0.0), jnp.sum)

    def bit_body(i, t):
        cand = t + jnp.left_shift(jnp.int32(1), 31 - i)
        return jnp.where(count(lambda kk: kk >= cand) >= TOPK, cand, t)

    thr = lax.fori_loop(0, 32, bit_body, jnp.full((1, Q_BLOCK), INT_MIN, I32))
    need = TOPK - count(lambda kk: kk > thr)
    n_eq = count(lambda kk: kk == thr)
    keys = key_s[0:nk, :]
    floor = jnp.maximum(thr, KEY_NEG_INF + 1)
    bias_s[0:nk, :] = jnp.where((keys >= floor) & (keys < KEY_POS_INF), 0.0, NEG_INF)
    has_tie = jnp.max(jnp.where((n_eq > need) & (thr > KEY_NEG_INF), 1.0, 0.0)) > 0.0

    @pl.when(has_tie)
    def _():
        tri = (_iota((KC, KC), 0) >= _iota((KC, KC), 1)).astype(BF16)
        run = jnp.zeros((1, Q_BLOCK), F32)
        for cc in range(nk // KC):
            kk = key_s[cc * KC:(cc + 1) * KC, :]
            eq = kk == thr
            eqf = jnp.where(eq, 1.0, 0.0)
            rank = jnp.dot(tri, eqf.astype(BF16), preferred_element_type=F32) + run
            ok = ((kk > thr) | (eq & (rank <= need))) & (kk > KEY_NEG_INF) & (kk < KEY_POS_INF)
            bias_s[cc * KC:(cc + 1) * KC, :] = jnp.where(ok, 0.0, NEG_INF)
            run = run + jnp.sum(eqf, axis=0, keepdims=True)

    b = bias_s[0:nk, :]
    bias2 = jnp.concatenate([b, b], axis=1)
    for g in range(B_KV_HEADS):
        pair = jnp.concatenate([q_t[(2 * g) * HEAD_DIM:(2 * g + 1) * HEAD_DIM, :],
                                q_t[(2 * g + 1) * HEAD_DIM:(2 * g + 2) * HEAD_DIM, :]], axis=1)
        s = jnp.dot(kbf_s[g, 0:nk, :], pair, preferred_element_type=F32) + bias2
        mx = _reduce_rows(s, jnp.max)
        pr = jnp.exp((s - mx).astype(BF16))
        acc = jnp.dot(vt_s[g, :, 0:nk], pr, preferred_element_type=F32)
        o = acc[0:HEAD_DIM] / acc[HEAD_DIM:HEAD_DIM + 1]
        ot_s[(2 * g) * HEAD_DIM:(2 * g + 1) * HEAD_DIM, :] = o[:, :Q_BLOCK]
        ot_s[(2 * g + 1) * HEAD_DIM:(2 * g + 2) * HEAD_DIM, :] = o[:, Q_BLOCK:]


def _dsa_prompt_kernel(q_ref, iq_ref, ikwq_ref, ikw_ref, k_ref, v_ref, y_ref,
                       kbf_s, vt_s, ikbf_s, key_s, bias_s, ot_s, *, T, classes):
    qb = pl.program_id(1)

    @pl.when(qb == 0)
    def _():
        ikbf_s[...] = ikw_ref[0][:, :IDX_DIM].astype(BF16)
        vt_s[...] = jnp.ones(vt_s.shape, BF16)
        for g in range(B_KV_HEADS):
            kbf_s[g] = k_ref[0][:, g * HEAD_DIM:(g + 1) * HEAD_DIM].astype(BF16)
        for cc in range(T // KC):
            vt = v_ref[0, cc * KC:(cc + 1) * KC, :].T.astype(BF16)
            for g in range(B_KV_HEADS):
                vt_s[g, 0:HEAD_DIM, cc * KC:(cc + 1) * KC] = vt[g * HEAD_DIM:(g + 1) * HEAD_DIM]

    iq_t = iq_ref[0].T.astype(BF16)
    w_t = ikwq_ref[0].T[IDX_DIM:IDX_DIM + IDX_HEADS, :] * (IDX_DIM ** -0.5 * IDX_HEADS ** -0.5)
    q_t = (q_ref[0] * (HEAD_DIM ** -0.5)).T.astype(BF16)

    per_class = (T // Q_BLOCK) // classes
    for cls in range(classes):
        nk = (cls + 1) * per_class * Q_BLOCK

        @pl.when(qb // per_class == cls)
        def _(nk=nk):
            _dsa_prompt_block(nk, qb, iq_t, w_t, q_t, kbf_s, vt_s, ikbf_s, key_s, bias_s, ot_s)

    y_ref[0] = ot_s[...].T


def _dsa_prompt(q3, iq3, ikw3, k3, v3):
    bsz, t, _ = q3.shape
    nqb = t // Q_BLOCK
    blk = lambda b, j: (b, j, 0)
    full = lambda b, j: (b, 0, 0)
    classes = math.gcd(nqb, KEY_CLASSES)
    assert (nqb // classes) * Q_BLOCK % KC == 0
    kern = functools.partial(_dsa_prompt_kernel, T=t, classes=classes)
    return pl.pallas_call(
        kern,
        grid=(bsz, nqb),
        in_specs=[pl.BlockSpec((1, Q_BLOCK, B_WIDTH), blk),
                  pl.BlockSpec((1, Q_BLOCK, IQ_WIDTH), blk),
                  pl.BlockSpec((1, Q_BLOCK, LANES), blk),
                  pl.BlockSpec((1, t, LANES), full),
                  pl.BlockSpec((1, t, KV_WIDTH), full),
                  pl.BlockSpec((1, t, KV_WIDTH), full)],
        out_specs=pl.BlockSpec((1, Q_BLOCK, B_WIDTH), blk),
        out_shape=jax.ShapeDtypeStruct((bsz, t, B_WIDTH), F32),
        scratch_shapes=[pltpu.VMEM((B_KV_HEADS, t, HEAD_DIM), BF16),
                        pltpu.VMEM((B_KV_HEADS, HEAD_DIM + BF16_ROWS, t), BF16),
                        pltpu.VMEM((t, IDX_DIM), BF16),
                        pltpu.VMEM((t, Q_BLOCK), I32),
                        pltpu.VMEM((t, Q_BLOCK), F32),
                        pltpu.VMEM((B_WIDTH, Q_BLOCK), F32)],
        compiler_params=_cparams("parallel", "arbitrary"),
    )(q3, iq3, ikw3, ikw3, k3, v3)


PG = 16


def _page_pipeline(pt_ref, base, steps, arrays):
    b = pl.program_id(0)
    j = pl.program_id(1)
    n = b * steps + j
    slot = n % 2

    def copies(bq, jq, sl):
        return [pltpu.make_async_copy(hbm.at[base + pt_ref[bq, jq * PG + i]], buf.at[sl, i], sem.at[sl])
                for hbm, buf, sem in arrays for i in range(PG)]

    @pl.when(n == 0)
    def _():
        for cp in copies(b, j, slot):
            cp.start()

    wrap = j + 1 == steps

    @pl.when(n + 1 < pl.num_programs(0) * steps)
    def _():
        for cp in copies(jnp.where(wrap, b + 1, b), jnp.where(wrap, 0, j + 1), 1 - slot):
            cp.start()

    for cp in copies(b, j, slot):
        cp.wait()
    return slot


def _sidx_kernel(pt_ref, iqf_ref, wcol_ref, pages_hbm, out_ref, buf, sem, *, Ts, base, steps):
    slot = _page_pipeline(pt_ref, base, steps, [(pages_hbm, buf, sem)])
    iqf = iqf_ref[0].astype(BF16)
    wcol = wcol_ref[0]
    for i in range(PG):
        s = _dot(iqf, buf[slot, i])
        r = jnp.maximum(s, 0.0) * wcol
        out_ref[0, i] = jnp.sum(r.reshape(IDX_HEADS, Ts, PAGE), axis=0)


def _sidx(page_table, iqf, wcol, cache_idx_kt, base, Ts):
    db, n_pages = page_table.shape
    steps = n_pages // PG
    grid_spec = pltpu.PrefetchScalarGridSpec(
        num_scalar_prefetch=1,
        grid=(db, steps),
        in_specs=[pl.BlockSpec((1, IDX_HEADS * Ts, IDX_DIM), lambda b, j, pt: (b, 0, 0)),
                  pl.BlockSpec((1, IDX_HEADS * Ts, 1), lambda b, j, pt: (b, 0, 0)),
                  pl.BlockSpec(memory_space=pl.ANY)],
        out_specs=pl.BlockSpec((1, PG, Ts, PAGE), lambda b, j, pt: (b, j, 0, 0)),
        scratch_shapes=[pltpu.VMEM((2, PG, IDX_DIM, PAGE), F32), pltpu.SemaphoreType.DMA((2,))],
    )
    return pl.pallas_call(
        functools.partial(_sidx_kernel, Ts=Ts, base=base, steps=steps),
        grid_spec=grid_spec,
        out_shape=jax.ShapeDtypeStruct((db, n_pages, Ts, PAGE), F32),
        compiler_params=_cparams("arbitrary", "arbitrary"),
    )(page_table, iqf, wcol, cache_idx_kt)


def _ssel_kernel(sc_ref, iqf_ref, wcol_ref, iknew_ref, bias_ref, key_s, *, Ts, NT, NB):
    vis = _iota((Ts, PAGE), 1) <= _iota((Ts, PAGE), 0)
    for bb in range(NB):
        key_s[bb, 0:NT] = _order_key(sc_ref[bb])
        s = _dot_nt(iqf_ref[bb], iknew_ref[bb])
        r = jnp.maximum(s, 0.0) * wcol_ref[bb]
        snew = jnp.sum(r.reshape(IDX_HEADS, Ts, PAGE), axis=0)
        key_s[bb, NT] = _order_key(jnp.where(vis, snew, NEG_INF))

    def count(mask):
        ones = jnp.where(mask, 1.0, 0.0)
        part = ones[NT]
        if NT % 8 == 0:
            part = part + jnp.sum(jnp.sum(ones[0:NT].reshape(NT // 8, 8, Ts, PAGE), axis=0), axis=0)
        else:
            part = part + jnp.sum(ones[0:NT], axis=0)
        return jnp.sum(part, axis=1, keepdims=True)

    def bit_body(i, ts):
        out = []
        for bb in range(NB):
            cand = ts[bb] + jnp.left_shift(jnp.int32(1), 31 - i)
            cnt = count(key_s[bb] >= cand[None])
            out.append(jnp.where(cnt >= TOPK, cand, ts[bb]))
        return tuple(out)

    thrs = lax.fori_loop(0, 32, bit_body, tuple(jnp.full((Ts, 1), INT_MIN, I32) for _ in range(NB)))
    for bb in range(NB):
        thr = thrs[bb]
        keys = key_s[bb]
        need = TOPK - count(keys > thr[None])
        n_eq = count(keys == thr[None])
        finite = (keys > KEY_NEG_INF) & (keys < KEY_POS_INF)
        bias_ref[bb] = jnp.where((keys >= thr[None]) & finite, 0.0, NEG_INF)
        has_tie = jnp.max(jnp.where((n_eq > need) & (thr > KEY_NEG_INF), 1.0, 0.0)) > 0.0

        @pl.when(has_tie)
        def _(bb=bb, thr=thr, need=need):
            triu = (_iota((PAGE, PAGE), 0) <= _iota((PAGE, PAGE), 1)).astype(BF16)

            def tile_body(j, run):
                kk = key_s[bb, j]
                eq = kk == thr
                eqf = jnp.where(eq, 1.0, 0.0)
                rank = jnp.dot(eqf.astype(BF16), triu, preferred_element_type=F32) + run
                sel = (kk > thr) | (eq & (rank <= need))
                ok = sel & (kk > KEY_NEG_INF) & (kk < KEY_POS_INF)
                bias_ref[bb, j] = jnp.where(ok, 0.0, NEG_INF)
                return run + jnp.sum(eqf, axis=1, keepdims=True)

            lax.fori_loop(0, NT + 1, tile_body, jnp.zeros((Ts, 1), F32))


def _ssel(scores, iqf, wcol, iknew):
    db, nt, ts, _ = scores.shape
    NB = math.gcd(db, SEQ_ROWS_PER_STEP)
    b3 = lambda b: (b, 0, 0)
    b4 = lambda b: (b, 0, 0, 0)
    return pl.pallas_call(
        functools.partial(_ssel_kernel, Ts=ts, NT=nt, NB=NB),
        grid=(db // NB,),
        in_specs=[pl.BlockSpec((NB, nt, ts, PAGE), b4),
                  pl.BlockSpec((NB, IDX_HEADS * ts, IDX_DIM), b3),
                  pl.BlockSpec((NB, IDX_HEADS * ts, 1), b3),
                  pl.BlockSpec((NB, PAGE, IDX_DIM), b3)],
        out_specs=pl.BlockSpec((NB, nt + 1, ts, PAGE), b4),
        out_shape=jax.ShapeDtypeStruct((db, nt + 1, ts, PAGE), F32),
        scratch_shapes=[pltpu.VMEM((NB, nt + 1, ts, PAGE), I32)],
        compiler_params=_cparams("parallel"),
    )(scores, iqf, wcol, iknew)


def _sattn_kernel(pt_ref, qbd_ref, bias_ref, biasn_ref, knew_ref, vnew_ref, k_hbm, v_hbm, y_ref,
                  m_s, l_s, acc_s, kbuf, vbuf, ksem, vsem, *, Ts, base, steps):
    slot = _page_pipeline(pt_ref, base, steps, [(k_hbm, kbuf, ksem), (v_hbm, vbuf, vsem)])
    j = pl.program_id(1)

    @pl.when(j == 0)
    def _():
        m_s[...] = jnp.full(m_s.shape, NEG_INF, F32)
        l_s[...] = jnp.zeros(l_s.shape, F32)
        acc_s[...] = jnp.zeros(acc_s.shape, F32)

    qbd = qbd_ref[0].astype(BF16)

    def rows(b):
        return jnp.concatenate([b] * B_HEADS, axis=0)

    def update(s, pv):
        m_old = m_s[...]
        m_new = jnp.maximum(m_old, jnp.max(s, axis=1, keepdims=True))
        m_safe = jnp.where(m_new == NEG_INF, 0.0, m_new)
        scale = jnp.exp(m_old - m_safe)
        pr = jnp.exp(s - m_safe)
        l_s[...] = scale * l_s[...] + jnp.sum(pr, axis=1, keepdims=True)
        acc_s[...] = scale * acc_s[...] + pv(pr)
        m_s[...] = m_new

    s = jnp.concatenate(
        [_dot(qbd, kbuf[slot, i]) + rows(bias_ref[0, i]) for i in range(PG)],
        axis=1)

    def pv(pr):
        acc = jnp.zeros((B_HEADS * Ts, KV_WIDTH), F32)
        for i in range(PG):
            acc = acc + _dot_nt(pr[:, i * PAGE:(i + 1) * PAGE], vbuf[slot, i])
        return acc

    update(s, pv)

    @pl.when(j == pl.num_programs(1) - 1)
    def _():
        sn = _dot(qbd, knew_ref[0]) + rows(biasn_ref[0, 0])
        update(sn, lambda pr: _dot_nt(pr, vnew_ref[0]))
        o = acc_s[...] / l_s[...]
        for h in range(B_HEADS):
            g = h // (B_HEADS // B_KV_HEADS)
            y_ref[0, :, h * HEAD_DIM:(h + 1) * HEAD_DIM] = o[h * Ts:(h + 1) * Ts, g * HEAD_DIM:(g + 1) * HEAD_DIM]


def _sattn(page_table, qbd, bias, knew_t, vnew_t, cache_kt, cache_vt, base, Ts):
    db, n_pages = page_table.shape
    steps = n_pages // PG
    b3 = lambda b, j, pt: (b, 0, 0)
    nrows = B_HEADS * Ts
    grid_spec = pltpu.PrefetchScalarGridSpec(
        num_scalar_prefetch=1,
        grid=(db, steps),
        in_specs=[pl.BlockSpec((1, nrows, KV_WIDTH), b3),
                  pl.BlockSpec((1, PG, Ts, PAGE), lambda b, j, pt: (b, j, 0, 0)),
                  pl.BlockSpec((1, 1, Ts, PAGE), lambda b, j, pt: (b, n_pages, 0, 0)),
                  pl.BlockSpec((1, KV_WIDTH, PAGE), b3),
                  pl.BlockSpec((1, KV_WIDTH, PAGE), b3),
                  pl.BlockSpec(memory_space=pl.ANY),
                  pl.BlockSpec(memory_space=pl.ANY)],
        out_specs=pl.BlockSpec((1, Ts, B_WIDTH), b3),
        scratch_shapes=[pltpu.VMEM((nrows, 1), F32),
                        pltpu.VMEM((nrows, 1), F32),
                        pltpu.VMEM((nrows, KV_WIDTH), F32),
                        pltpu.VMEM((2, PG, KV_WIDTH, PAGE), F32),
                        pltpu.VMEM((2, PG, KV_WIDTH, PAGE), F32),
                        pltpu.SemaphoreType.DMA((2,)),
                        pltpu.SemaphoreType.DMA((2,))],
    )
    return pl.pallas_call(
        functools.partial(_sattn_kernel, Ts=Ts, base=base, steps=steps),
        grid_spec=grid_spec,
        out_shape=jax.ShapeDtypeStruct((db, Ts, B_WIDTH), F32),
        compiler_params=_cparams("arbitrary", "arbitrary"),
    )(page_table, qbd, bias, bias, knew_t, vnew_t, cache_kt, cache_vt)


def _dsa_sample(q3, iq3, ikw3, k3, v3, cache_kt, cache_vt, cache_idx_kt, page_table, base):
    db, ts, _ = q3.shape
    iqf = iq3.reshape(db, ts, IDX_HEADS, IDX_DIM).transpose(0, 2, 1, 3).reshape(db, IDX_HEADS * ts, IDX_DIM)
    iw = ikw3[:, :, IDX_DIM:IDX_DIM + IDX_HEADS] * (IDX_DIM ** -0.5 * IDX_HEADS ** -0.5)
    wcol = iw.transpose(0, 2, 1).reshape(db, IDX_HEADS * ts, 1)
    qh = (q3 * (HEAD_DIM ** -0.5)).reshape(db, ts, B_HEADS, HEAD_DIM).transpose(0, 2, 1, 3)
    own = (jnp.arange(B_HEADS)[:, None] // (B_HEADS // B_KV_HEADS)) == jnp.arange(B_KV_HEADS)[None, :]
    qbd = jnp.where(own[None, :, None, :, None], qh[:, :, :, None, :], 0.0)
    qbd = qbd.reshape(db, B_HEADS * ts, KV_WIDTH)
    pad = lambda a: jnp.pad(a, ((0, 0), (0, PAGE - ts), (0, 0)))
    iknew = pad(ikw3[:, :, :IDX_DIM])
    scores = _sidx(page_table, iqf, wcol, cache_idx_kt, base, ts)
    bias = _ssel(scores, iqf, wcol, iknew)
    knew_t = pad(k3).transpose(0, 2, 1)
    vnew_t = pad(v3).transpose(0, 2, 1)
    return _sattn(page_table, qbd, bias, knew_t, vnew_t, cache_kt, cache_vt, base, ts)


def _ffn_kernel(x_ref, ya_ref, yb_ref, yc_ref, wo_ref, gf_ref, wg_ref, wu_ref, wd_ref, gfin_ref,
                o_ref, *, final):
    mixed = (jnp.dot(ya_ref[...].astype(BF16), wo_ref[0:A_WIDTH, :], preferred_element_type=F32)
             + jnp.dot(yb_ref[...].astype(BF16), wo_ref[A_WIDTH:A_WIDTH + B_WIDTH, :],
                       preferred_element_type=F32)
             + jnp.dot(yc_ref[...].astype(BF16), wo_ref[A_WIDTH + B_WIDTH:, :],
                       preferred_element_type=F32))
    x1 = x_ref[...] + mixed
    ms = jnp.mean(x1 * x1, axis=-1, keepdims=True)
    hf = (x1 * lax.rsqrt(ms + NORM_EPS) * gf_ref[...]).astype(BF16)
    gate = jnp.dot(hf, wg_ref[...], preferred_element_type=F32)
    up = jnp.dot(hf, wu_ref[...], preferred_element_type=F32)
    act = (gate * _sigmoid(gate) * up).astype(BF16)
    x2 = x1 + jnp.dot(act, wd_ref[...], preferred_element_type=F32)
    if final:
        ms2 = jnp.mean(x2 * x2, axis=-1, keepdims=True)
        x2 = x2 * lax.rsqrt(ms2 + NORM_EPS) * gfin_ref[...]
    o_ref[...] = x2


def _ffn(x2d, ya, yb, yc, wo, gf, wg, wu, wd, gfin, final, tm):
    n, d = x2d.shape
    dff = wg.shape[1]
    row = lambda i: (i, 0)
    fixed = lambda i: (0, 0)
    once = dict(pipeline_mode=pl.Buffered(1))
    return pl.pallas_call(
        functools.partial(_ffn_kernel, final=final),
        grid=(n // tm,),
        in_specs=[pl.BlockSpec((tm, d), row), pl.BlockSpec((tm, A_WIDTH), row),
                  pl.BlockSpec((tm, B_WIDTH), row), pl.BlockSpec((tm, C_WIDTH), row),
                  pl.BlockSpec((d, d), fixed, **once), pl.BlockSpec((1, d), fixed),
                  pl.BlockSpec((d, dff), fixed, **once), pl.BlockSpec((d, dff), fixed, **once),
                  pl.BlockSpec((dff, d), fixed, **once), pl.BlockSpec((1, d), fixed)],
        out_specs=pl.BlockSpec((tm, d), row),
        out_shape=jax.ShapeDtypeStruct((n, d), F32),
        compiler_params=_cparams("parallel"),
    )(x2d, ya, yb, yc, wo, gf, wg, wu, wd, gfin)


def _rope_tables(pos):
    half = ROT_DIM // 2
    inv = ROPE_THETA ** (-jnp.arange(half, dtype=F32) / half)
    ang = pos.astype(F32)[:, None] * inv[None, :]
    j = jnp.arange(LANES) % HEAD_DIM
    cos = jnp.cos(ang)[:, j % half]
    sin = jnp.sin(ang)[:, j % half]
    cos_t = jnp.where(j[None, :] < ROT_DIM, cos, 1.0)
    sina = jnp.where(j[None, :] < half, -sin, 0.0)
    sinb = jnp.where((j[None, :] >= half) & (j[None, :] < ROT_DIM), sin, 0.0)
    return cos_t, sina, sinb


def _pad_w_in(w):
    d = w.shape[0]
    z = lambda n: jnp.zeros((d, n), w.dtype)
    gates = jnp.repeat(w[:, 4 * A_WIDTH:A_PROJ], HEAD_DIM, axis=1)
    return jnp.concatenate([w[:, :4 * A_WIDTH], gates,
                            w[:, A_PROJ:A_PROJ + B_PROJ], z(B_PAD - B_PROJ),
                            w[:, A_PROJ + B_PROJ:]], axis=1).astype(BF16)


def _layer(x3, l, tabs, mstate, shift0, s0, attn_fn, p, wts, final, tm):
    bsz, t, d = x3.shape
    n = bsz * t
    pa, q, k, v, iq, ikw, pc = _inproj(x3.reshape(n, d), p['norm_mix'][l].reshape(1, d), wts['w_in'][l], tabs, tm)
    r3 = lambda a: a.reshape(bsz, t, a.shape[-1])
    y_a, c_new, n_new, m_new = _mlstm(r3(pa), *mstate, p['mlstm_gate_b'][l], p['mlstm_norm'][l])
    y_b = attn_fn(r3(q), r3(iq), r3(ikw), r3(k), r3(v))
    y_c, shift, s_new = _rwkv(r3(pc), shift0, s0, p, l)
    x_new = _ffn(x3.reshape(n, d), y_a.reshape(n, -1), y_b.reshape(n, -1), y_c.reshape(n, -1),
                 wts['w_out'][l], p['norm_ffn'][l].reshape(1, d), wts['w_gate'][l], wts['w_up'][l],
                 wts['w_down'][l], p['norm_final'].reshape(1, d), final, tm)
    new = (k.reshape(bsz, t, B_KV_HEADS, HEAD_DIM), v.reshape(bsz, t, B_KV_HEADS, HEAD_DIM),
           r3(ikw)[:, :, :IDX_DIM], c_new, n_new, m_new, shift, s_new)
    return x_new.reshape(bsz, t, d), new


def kernel(x_prompt, x_sample, cache_k, cache_v, cache_idx_k, state_mlstm_C, state_mlstm_n, state_mlstm_m, state_rwkv_shift, state_rwkv_S, page_table, norm_mix, w_in, mlstm_gate_b, mlstm_norm, rwkv_mu, rwkv_w0, rwkv_w2, rwkv_a0, rwkv_a2, rwkv_g2, rwkv_k_k, rwkv_k_a, rwkv_r_k, rwkv_ln_w, rwkv_ln_b, w_out, norm_ffn, w_gate, w_up, w_down, norm_final):
    p = dict(norm_mix=norm_mix, mlstm_gate_b=mlstm_gate_b, mlstm_norm=mlstm_norm,
             rwkv_mu=rwkv_mu, rwkv_w0=rwkv_w0, rwkv_w2=rwkv_w2, rwkv_a0=rwkv_a0, rwkv_a2=rwkv_a2,
             rwkv_g2=rwkv_g2, rwkv_k_k=rwkv_k_k, rwkv_k_a=rwkv_k_a, rwkv_r_k=rwkv_r_k,
             rwkv_ln_w=rwkv_ln_w, rwkv_ln_b=rwkv_ln_b, norm_ffn=norm_ffn, norm_final=norm_final)
    depth = w_in.shape[0]
    wts = dict(w_in=[_pad_w_in(w_in[l]) for l in range(depth)],
               w_out=w_out.astype(BF16), w_gate=w_gate.astype(BF16),
               w_up=w_up.astype(BF16), w_down=w_down.astype(BF16))
    bp, tp, _ = x_prompt.shape
    db, ts, _ = x_sample.shape
    past = page_table.shape[1] * PAGE
    tabs_p = _rope_tables(jnp.arange(tp, dtype=I32))
    tabs_s = tuple(jnp.tile(a, (db, 1)) for a in _rope_tables(past + jnp.arange(ts, dtype=I32)))
    tm_p = min(256, bp * tp)
    tm_s = db * ts
    zc = jnp.zeros((bp, A_HEADS, HEAD_DIM, HEAD_DIM), F32)
    zn = jnp.zeros((bp, A_HEADS, HEAD_DIM), F32)
    zm = jnp.zeros((bp, A_HEADS), F32)
    zshift = jnp.zeros((bp, C_PROJ), F32)
    zs = jnp.zeros((bp, C_HEADS, HEAD_DIM, HEAD_DIM), F32)
    n_pool = cache_k.shape[1]
    cache_kt = cache_k.transpose(0, 1, 3, 4, 2).reshape(depth * n_pool, KV_WIDTH, PAGE)
    cache_vt = cache_v.transpose(0, 1, 3, 4, 2).reshape(depth * n_pool, KV_WIDTH, PAGE)
    cache_idx_kt = cache_idx_k.transpose(0, 1, 3, 2).reshape(depth * n_pool, IDX_DIM, PAGE)
    xp, xs = x_prompt, x_sample
    new_p, new_s = [], []
    for l in range(depth):
        final = l == depth - 1
        xp, st = _layer(xp, l, tabs_p, (zc, zn, zm), zshift, zs, _dsa_prompt, p, wts, final, tm_p)
        new_p.append(st)
        attn_s = functools.partial(_dsa_sample, cache_kt=cache_kt, cache_vt=cache_vt,
                                   cache_idx_kt=cache_idx_kt, page_table=page_table, base=l * n_pool)
        xs, st = _layer(xs, l, tabs_s, (state_mlstm_C[l], state_mlstm_n[l], state_mlstm_m[l]),
                        state_rwkv_shift[l], state_rwkv_S[l], attn_s, p, wts, final, tm_s)
        new_s.append(st)
    stack = lambda states, i: jnp.stack([st[i] for st in states])
    outs_p = [stack(new_p, i) for i in range(8)]
    outs_s = [stack(new_s, i) for i in range(8)]
    return (xp, xs, *outs_p, *outs_s)
```

```python
import functools
import math

import jax
import jax.numpy as jnp
from jax import lax
from jax.experimental import pallas as pl
from jax.experimental.pallas import tpu as pltpu

F32 = jnp.float32
BF16 = jnp.bfloat16
I32 = jnp.int32

HEAD_DIM = 64
A_HEADS = 4
B_HEADS = 8
B_KV_HEADS = 4
C_HEADS = 4
IDX_HEADS = 8
IDX_DIM = 64
TOPK = 256
Q_BLOCK = 128
ROT_DIM = 16
ROPE_THETA = 500000.0
MLSTM_CHUNK = 64
RWKV_CHUNK = 64
SEQ_ROWS_PER_STEP = 8
RWKV_LN_EPS = 64e-5
NORM_EPS = 1e-6
PAGE = 128

A_WIDTH = A_HEADS * HEAD_DIM
B_WIDTH = B_HEADS * HEAD_DIM
KV_WIDTH = B_KV_HEADS * HEAD_DIM
C_WIDTH = C_HEADS * HEAD_DIM
IQ_WIDTH = IDX_HEADS * IDX_DIM
A_PROJ = 4 * A_WIDTH + 2 * A_HEADS
B_PROJ = B_WIDTH + 2 * KV_WIDTH + IQ_WIDTH + IDX_DIM + IDX_HEADS
C_PROJ = 3 * C_WIDTH + 64 + 64 + 128

LANES = 128
A_PAD = 6 * A_WIDTH
B_PAD = B_PROJ + (LANES - (IDX_DIM + IDX_HEADS))
P_PAD = A_PAD + B_PAD + C_PROJ

VMEM_LIMIT = 56 * 1024 * 1024

INT_MIN = -2147483648
KEY_NEG_INF = -2139095040
KEY_POS_INF = 2139095040
NEG_INF = float("-inf")
PAD_LOG_GATE = -1e30
SUBLANES = 8
BF16_ROWS = 16


def _cparams(*sem):
    return pltpu.CompilerParams(dimension_semantics=sem, vmem_limit_bytes=VMEM_LIMIT)


def _dot(a, b):
    return jnp.dot(a.astype(BF16), b.astype(BF16), preferred_element_type=F32)


def _dot_nt(a, b):
    return lax.dot_general(a.astype(BF16), b.astype(BF16), (((1,), (1,)), ((), ())),
                           preferred_element_type=F32)


def _split3(a):
    hi = a.astype(BF16)
    r = a - hi.astype(F32)
    mid = r.astype(BF16)
    lo = (r - mid.astype(F32)).astype(BF16)
    return hi, mid, lo


_NN = (((1,), (0,)), ((), ()))
_NT = (((1,), (1,)), ((), ()))
_TN = (((0,), (0,)), ((), ()))


def _dg(a, b, dims):
    return lax.dot_general(a, b, dims, preferred_element_type=F32)


def _dot01_left(m01, x):
    m = m01.astype(BF16)
    x1, x2, x3 = _split3(x)
    return _dg(m, x1, _NN) + (_dg(m, x2, _NN) + _dg(m, x3, _NN))


def _dot01_right(x, m01):
    m = m01.astype(BF16)
    x1, x2, x3 = _split3(x)
    return _dg(x1, m, _NN) + (_dg(x2, m, _NN) + _dg(x3, m, _NN))


def _sigmoid(x):
    return 1.0 / (1.0 + jnp.exp(-x))


def _softplus(x):
    return jnp.maximum(x, 0.0) + jnp.log1p(jnp.exp(-jnp.abs(x)))


def _iota(shape, dim):
    return lax.broadcasted_iota(I32, shape, dim)


def _round_robin(gens):
    gens = list(gens)
    while gens:
        alive = []
        for g in gens:
            try:
                next(g)
                alive.append(g)
            except StopIteration:
                pass
        gens = alive


def _reduce_rows(x, op):
    n, w = x.shape
    group = 8 * SUBLANES
    if n > group and n % group == 0:
        x = op(x.reshape(n // group, group, w), axis=0)
    return op(x, axis=0, keepdims=True)


def _rope_tile(xt, cos, sina, sinb):
    up = pltpu.roll(xt, LANES - ROT_DIM // 2, axis=1)
    dn = pltpu.roll(xt, ROT_DIM // 2, axis=1)
    return xt * cos + up * sina + dn * sinb


def _inproj_kernel(x_ref, g_ref, w_ref, cos_ref, sina_ref, sinb_ref,
                   pa_ref, q_ref, k_ref, v_ref, iq_ref, ikw_ref, pc_ref):
    x = x_ref[...]
    ms = jnp.mean(x * x, axis=-1, keepdims=True)
    h = (x * lax.rsqrt(ms + NORM_EPS) * g_ref[...]).astype(BF16)
    cos = cos_ref[...]
    sina = sina_ref[...]
    sinb = sinb_ref[...]

    def proj(lo, width):
        return jnp.dot(h, w_ref[:, lo:lo + width], preferred_element_type=F32)

    def rope(x2):
        tiles = [_rope_tile(x2[:, j * LANES:(j + 1) * LANES], cos, sina, sinb)
                 for j in range(x2.shape[1] // LANES)]
        return tiles[0] if len(tiles) == 1 else jnp.concatenate(tiles, axis=1)

    pa_ref[...] = proj(0, A_PAD)
    b0 = A_PAD
    q_ref[...] = rope(proj(b0, B_WIDTH))
    k_ref[...] = rope(proj(b0 + B_WIDTH, KV_WIDTH))
    v_ref[...] = proj(b0 + B_WIDTH + KV_WIDTH, KV_WIDTH)
    iq_ref[...] = rope(proj(b0 + B_WIDTH + 2 * KV_WIDTH, IQ_WIDTH))
    ikw = proj(b0 + B_WIDTH + 2 * KV_WIDTH + IQ_WIDTH, LANES)
    lane = _iota(ikw.shape, 1)
    ikw_ref[...] = jnp.where(lane < IDX_DIM, _rope_tile(ikw, cos, sina, sinb), ikw)
    pc_ref[...] = proj(A_PAD + B_PAD, C_PROJ)


def _inproj(x2d, g, w_bf, tabs, tm):
    n, d = x2d.shape
    cos, sina, sinb = tabs
    tab_blocks = cos.shape[0] // tm
    row = lambda i: (i, 0)
    fixed = lambda i: (0, 0)
    tab = lambda i: (i % tab_blocks, 0)
    widths = (A_PAD, B_WIDTH, KV_WIDTH, KV_WIDTH, IQ_WIDTH, LANES, C_PROJ)
    return pl.pallas_call(
        _inproj_kernel,
        grid=(n // tm,),
        in_specs=[pl.BlockSpec((tm, d), row), pl.BlockSpec((1, d), fixed),
                  pl.BlockSpec((d, P_PAD), fixed),
                  pl.BlockSpec((tm, LANES), tab), pl.BlockSpec((tm, LANES), tab),
                  pl.BlockSpec((tm, LANES), tab)],
        out_specs=[pl.BlockSpec((tm, w), row) for w in widths],
        out_shape=[jax.ShapeDtypeStruct((n, w), F32) for w in widths],
        compiler_params=_cparams("parallel"),
    )(x2d, g, w_bf, cos, sina, sinb)


def _mlstm_kernel(pa_ref, c0_ref, n0_ref, m0_ref, gb_ref, gn_ref,
                  y_ref, cout_ref, nout_ref, mout_ref, c_s, n_s, m_s, *, L, Lp, NB):
    c = pl.program_id(1)
    W = A_WIDTH
    HL = A_HEADS * Lp

    @pl.when(c == 0)
    def _():
        n_s[...] = n0_ref[...]
        m_s[...] = m0_ref[...]
        c_s[...] = jnp.zeros((NB, W, W), F32)
        for bb in range(NB):
            for h in range(A_HEADS):
                sl = slice(h * HEAD_DIM, (h + 1) * HEAD_DIM)
                c_s[bb, sl, sl] = c0_ref[bb, h]

    rowi = _iota((Lp, W), 0)
    valid = rowi < L
    tri = _iota((Lp, Lp), 0) >= _iota((Lp, Lp), 1)
    ones_ll = jnp.ones((Lp, Lp), F32)
    key_of_lane = _iota((Lp, HL), 1) % Lp
    eye_t = _iota((Lp, HL), 0) == key_of_lane
    causal_t = _iota((Lp, HL), 0) >= key_of_lane
    lane_h = _iota((Lp, W), 1) // HEAD_DIM
    same_head = _iota((W, W), 0) // HEAD_DIM == _iota((W, W), 1) // HEAD_DIM

    def stack(x):
        return jnp.concatenate([jnp.where(lane_h == h, x, 0.0) for h in range(A_HEADS)], axis=0)

    def head_sum(x):
        return _dot01_right(x, same_head)

    def row(bb):
        pa = pa_ref[bb]
        if Lp > L:
            pa = jnp.concatenate([pa, jnp.zeros((Lp - L, pa.shape[1]), F32)], axis=0)
        q = pa[:, 0:W]
        k = pa[:, W:2 * W] * (HEAD_DIM ** -0.5)
        v = pa[:, 2 * W:3 * W]
        o = pa[:, 3 * W:4 * W]
        gi = pa[:, 4 * W:5 * W] + gb_ref[:, 0:W]
        gf = pa[:, 5 * W:6 * W] + gb_ref[:, W:2 * W]
        log_i = jnp.where(valid, gi, PAD_LOG_GATE)
        log_f = jnp.where(valid, -_softplus(-gf), 0.0)
        k_st = stack(k).astype(BF16)
        v_st = stack(v).astype(BF16)
        q_bf = q.astype(BF16)
        yield
        b = _dot01_left(tri, log_f)
        qk = _dg(q_bf, k_st, _NT)
        c_bd = c_s[bb]
        n_row = n_s[bb]
        qc = _dg(q_bf, c_bd.astype(BF16), _NT)
        qn = head_sum(q * n_row)
        yield
        ib = log_i - b
        cm = ib
        sh = 1
        while sh < Lp:
            cm = jnp.where(rowi >= sh, jnp.maximum(cm, pltpu.roll(cm, sh, axis=0)), cm)
            sh *= 2
        m_prev = m_s[bb]
        mx = jnp.maximum(m_prev, cm)
        m_t = b + mx
        w_inter = jnp.exp(m_prev - mx)
        ib_diag = jnp.where(eye_t, ib, 0.0)
        mx_last = mx[Lp - 1:Lp, :]
        wk = jnp.exp(ib - mx_last)
        dec = jnp.exp(m_prev - mx_last)
        vw = (v * wk).astype(BF16)
        yield
        ib_row = _dot01_left(ones_ll, ib_diag)
        c_upd = _dg(vw, k.astype(BF16), _TN)
        yield
        c_s[bb] = dec * c_bd + jnp.where(same_head, c_upd, 0.0)
        n_s[bb] = dec * n_row + jnp.sum(k * wk, axis=0, keepdims=True)
        m_s[bb] = b[Lp - 1:Lp, :] + mx_last
        pmat = jnp.where(causal_t, jnp.exp(ib_row - mx), 0.0)
        s_all = qk * pmat
        s_bf = s_all.astype(BF16)
        yield
        num = _dg(s_bf, v_st, _NN) + w_inter * qc
        den = head_sum(s_all) + w_inter * qn
        yield
        hh = num / jnp.maximum(jnp.abs(den), jnp.exp(-m_t))
        hsq = hh * hh
        yield
        ms = head_sum(hsq) * (1.0 / HEAD_DIM)
        yield
        hn = hh * lax.rsqrt(ms + NORM_EPS) * gn_ref[...]
        y_ref[bb] = (_sigmoid(o) * hn)[:L]

    _round_robin([row(bb) for bb in range(NB)])

    @pl.when(c == pl.num_programs(1) - 1)
    def _():
        nout_ref[...] = n_s[...]
        mout_ref[...] = m_s[...]
        for bb in range(NB):
            for h in range(A_HEADS):
                sl = slice(h * HEAD_DIM, (h + 1) * HEAD_DIM)
                cout_ref[bb, h] = c_s[bb, sl, sl]


def _mlstm(pa3, c0, n0, m0, gate_b, gnorm):
    bsz, t, _ = pa3.shape
    L = min(t, MLSTM_CHUNK)
    Lp = MLSTM_CHUNK
    nc = t // L
    assert Lp == HEAD_DIM
    NB = math.gcd(bsz, SEQ_ROWS_PER_STEP)
    gb = jnp.repeat(gate_b, HEAD_DIM).reshape(1, 2 * A_WIDTH)
    m0_rep = jnp.repeat(m0, HEAD_DIM, axis=1).reshape(bsz, 1, A_WIDTH)
    kern = functools.partial(_mlstm_kernel, L=L, Lp=Lp, NB=NB)
    st4 = lambda b, c: (b, 0, 0, 0)
    st3 = lambda b, c: (b, 0, 0)
    fixed = lambda b, c: (0, 0)
    y, cn, nn, mn = pl.pallas_call(
        kern,
        grid=(bsz // NB, nc),
        in_specs=[pl.BlockSpec((NB, L, A_PAD), lambda b, c: (b, c, 0)),
                  pl.BlockSpec((NB, A_HEADS, HEAD_DIM, HEAD_DIM), st4),
                  pl.BlockSpec((NB, 1, A_WIDTH), st3),
                  pl.BlockSpec((NB, 1, A_WIDTH), st3),
                  pl.BlockSpec((1, 2 * A_WIDTH), fixed),
                  pl.BlockSpec((1, A_WIDTH), fixed)],
        out_specs=[pl.BlockSpec((NB, L, A_WIDTH), lambda b, c: (b, c, 0)),
                   pl.BlockSpec((NB, A_HEADS, HEAD_DIM, HEAD_DIM), st4),
                   pl.BlockSpec((NB, 1, A_WIDTH), st3),
                   pl.BlockSpec((NB, 1, A_WIDTH), st3)],
        out_shape=[jax.ShapeDtypeStruct((bsz, t, A_WIDTH), F32),
                   jax.ShapeDtypeStruct((bsz, A_HEADS, HEAD_DIM, HEAD_DIM), F32),
                   jax.ShapeDtypeStruct((bsz, 1, A_WIDTH), F32),
                   jax.ShapeDtypeStruct((bsz, 1, A_WIDTH), F32)],
        scratch_shapes=[pltpu.VMEM((NB, A_WIDTH, A_WIDTH), F32),
                        pltpu.VMEM((NB, 1, A_WIDTH), F32),
                        pltpu.VMEM((NB, 1, A_WIDTH), F32)],
        compiler_params=_cparams("parallel", "arbitrary"),
    )(pa3, c0, n0.reshape(bsz, 1, A_WIDTH), m0_rep, gb, gnorm.reshape(1, A_WIDTH))
    return y, cn, nn.reshape(bsz, A_HEADS, HEAD_DIM), mn[:, 0, ::HEAD_DIM]


def _rwkv_kernel(pc_ref, shift0_ref, s0_ref, mu_ref, w0_ref, w2_ref, a0_ref, a2_ref, g2_ref,
                 kk_ref, ka_ref, rk_ref, lnw_ref, lnb_ref,
                 y_ref, shift_out_ref, s_out_ref, s_s, carry_s, *, L, C, NB):
    c = pl.program_id(1)
    last = c == pl.num_programs(1) - 1
    W = C_WIDTH
    HC = C_HEADS * C

    @pl.when(c == 0)
    def _():
        carry_s[...] = shift0_ref[...]
        s_s[...] = jnp.zeros((NB, W, W), F32)
        for bb in range(NB):
            for h in range(C_HEADS):
                sl = slice(h * HEAD_DIM, (h + 1) * HEAD_DIM)
                s_s[bb, sl, sl] = s0_ref[bb, h]

    rowc = _iota((C, C_PROJ), 0)
    lane_head = _iota((W, W), 0) // HEAD_DIM == _iota((W, W), 1) // HEAD_DIM
    tri = _iota((C, C), 0) >= _iota((C, C), 1)
    lane_h = _iota((C, W), 1) // HEAD_DIM
    vrow = _iota((C, W), 0) < L
    tt = _iota((HC, HC), 0) % C
    ii = _iota((HC, HC), 1) % C
    eye = (_iota((HC, HC), 0) == _iota((HC, HC), 1)).astype(F32)

    def head_sum(x):
        return _dot01_right(x, lane_head)

    def stack(x):
        return jnp.concatenate([jnp.where(lane_h == h, x, 0.0) for h in range(C_HEADS)], axis=0)

    def level_mask(s):
        return ((tt // s) % 2 == 1) & ((ii // s) % 2 == 0) & (tt // (2 * s) == ii // (2 * s))

    levels = []
    s = 1
    while s < C:
        levels.append(level_mask(s))
        s *= 2

    def row(bb):
        pc = pc_ref[bb]
        if C > L:
            pc = jnp.concatenate([pc, jnp.zeros((C - L, pc.shape[1]), F32)], axis=0)
        prev = jnp.where(rowc == 0, carry_s[bb], pltpu.roll(pc, 1, axis=0))
        carry_s[bb] = pc[L - 1:L, :]
        pcs = pc + mu_ref[...] * (prev - pc)
        cr = pcs[:, 0:W]
        ck = pcs[:, W:2 * W]
        cv = pcs[:, 2 * W:3 * W]
        cwl = pcs[:, 3 * W:3 * W + 64]
        cal = pcs[:, 3 * W + 64:3 * W + 128]
        cgl = pcs[:, 3 * W + 128:3 * W + 256]
        wlog = -_softplus(-(w0_ref[...] + _dot(jnp.tanh(cwl), w2_ref[...]))) - 0.5
        lw = -jnp.exp(wlog)
        a = _sigmoid(a0_ref[...] + _dot(cal, a2_ref[...]))
        g = _dot(_sigmoid(cgl), g2_ref[...])
        kk = ck * kk_ref[...]
        kkn = kk / jnp.maximum(jnp.sqrt(head_sum(kk * kk)), 1e-12)
        k2 = ck * (1.0 + (a - 1.0) * ka_ref[...])
        alpha = -kkn
        beta = kkn * a
        if C > L:
            zero = lambda z: jnp.where(vrow, z, 0.0)
            lw, alpha, beta, k2s, cvs, crs = zero(lw), zero(alpha), zero(beta), zero(k2), zero(cv), zero(cr)
        else:
            k2s, cvs, crs = k2, cv, cr

        logp = _dot01_left(tri, lw)
        p = jnp.exp(logp)
        pinv = jnp.exp(-logp)
        pprev = jnp.exp(logp - lw)
        p_last = p[C - 1:C, :]
        a_st, r_st = stack(alpha * pprev), stack(crs * p)
        b_st, k_st, v_st = stack(beta * pinv), stack(k2s * pinv), stack(cvs)
        a_bf, b_bf, k_bf, v_bf = a_st.astype(BF16), b_st.astype(BF16), k_st.astype(BF16), v_st.astype(BF16)
        ar = jnp.concatenate([a_bf, r_st.astype(BF16)], axis=0)
        yield
        gb = _dg(ar, b_bf, _NT)
        gk = _dg(ar, k_bf, _NT)
        kv = _dg(v_bf, k_bf, _TN)
        yield
        l_ab = jnp.where(tt > ii, gb[:HC], 0.0)
        l_rb = jnp.where(tt >= ii, gb[HC:], 0.0).astype(BF16)
        l_ak = jnp.where(tt > ii, gk[:HC], 0.0).astype(BF16)
        l_rk = jnp.where(tt >= ii, gk[HC:], 0.0).astype(BF16)
        lakv = _dg(l_ak, v_bf, _NN).astype(BF16)
        lrkv = _dg(l_rk, v_bf, _NN)

        x = eye + jnp.where(levels[0], l_ab, 0.0)
        for lvl in levels[1:]:
            e = jnp.where(lvl, l_ab, 0.0).astype(BF16)
            xb = x.astype(BF16)
            yield
            xe = _dg(xb, e, _NN).astype(BF16)
            yield
            x = x + _dg(xe, xb, _NN)

        xb = x.astype(BF16)
        yield
        a_t = _dg(xb, a_bf, _NN)
        v_t = _dg(xb, lakv, _NN)
        yield
        s_old = s_s[bb]
        s_bf = s_old.astype(BF16)
        a_t_bf = a_t.astype(BF16)
        u = _dg(a_t_bf, s_bf, _NT) + v_t
        r_q = (r_st + _dg(l_rb, a_t_bf, _NN)).astype(BF16)
        y0 = _dg(l_rb, v_t.astype(BF16), _NN) + lrkv
        yield
        y_st = _dg(r_q, s_bf, _NT) + y0
        s_new = (s_old + _dg(u.astype(BF16), b_bf, _TN) + kv) * p_last
        s_s[bb] = s_new
        yield
        yc = y_st[0:C]
        for h in range(1, C_HEADS):
            yc = yc + y_st[h * C:(h + 1) * C]

        inv_d = 1.0 / HEAD_DIM
        mean = head_sum(yc) * inv_d
        dlt = yc - mean
        var = head_sum(dlt * dlt) * inv_d
        ycn = dlt * lax.rsqrt(var + RWKV_LN_EPS) * lnw_ref[...] + lnb_ref[...]
        bonus = head_sum(cr * k2 * rk_ref[...]) * cv
        y_ref[bb] = ((ycn + bonus) * g)[:L]

    _round_robin([row(bb) for bb in range(NB)])

    @pl.when(last)
    def _():
        shift_out_ref[...] = carry_s[...]
        for bb in range(NB):
            for h in range(C_HEADS):
                sl = slice(h * HEAD_DIM, (h + 1) * HEAD_DIM)
                s_out_ref[bb, h] = s_s[bb, sl, sl]


def _rwkv(pc3, shift0, s0, p, l):
    bsz, t, _ = pc3.shape
    C = RWKV_CHUNK
    L = min(t, C)
    nc = t // L
    NB = math.gcd(bsz, SEQ_ROWS_PER_STEP)
    r1 = lambda a: a.reshape(1, -1)
    params = [r1(p['rwkv_mu'][l]), r1(p['rwkv_w0'][l]), p['rwkv_w2'][l], r1(p['rwkv_a0'][l]),
              p['rwkv_a2'][l], p['rwkv_g2'][l], r1(p['rwkv_k_k'][l]), r1(p['rwkv_k_a'][l]),
              r1(p['rwkv_r_k'][l]), r1(p['rwkv_ln_w'][l]), r1(p['rwkv_ln_b'][l])]
    fixed = lambda b, c: (0, 0)
    pspecs = [pl.BlockSpec(a.shape, fixed) for a in params]
    kern = functools.partial(_rwkv_kernel, L=L, C=C, NB=NB)
    y, shift, s_new = pl.pallas_call(
        kern,
        grid=(bsz // NB, nc),
        in_specs=[pl.BlockSpec((NB, L, C_PROJ), lambda b, c: (b, c, 0)),
                  pl.BlockSpec((NB, 1, C_PROJ), lambda b, c: (b, 0, 0)),
                  pl.BlockSpec((NB, C_HEADS, HEAD_DIM, HEAD_DIM), lambda b, c: (b, 0, 0, 0))] + pspecs,
        out_specs=[pl.BlockSpec((NB, L, C_WIDTH), lambda b, c: (b, c, 0)),
                   pl.BlockSpec((NB, 1, C_PROJ), lambda b, c: (b, 0, 0)),
                   pl.BlockSpec((NB, C_HEADS, HEAD_DIM, HEAD_DIM), lambda b, c: (b, 0, 0, 0))],
        out_shape=[jax.ShapeDtypeStruct((bsz, t, C_WIDTH), F32),
                   jax.ShapeDtypeStruct((bsz, 1, C_PROJ), F32),
                   jax.ShapeDtypeStruct((bsz, C_HEADS, HEAD_DIM, HEAD_DIM), F32)],
        scratch_shapes=[pltpu.VMEM((NB, C_WIDTH, C_WIDTH), F32), pltpu.VMEM((NB, 1, C_PROJ), F32)],
        compiler_params=_cparams("parallel", "arbitrary"),
    )(pc3, shift0.reshape(bsz, 1, C_PROJ), s0, *params)
    return y, shift.reshape(bsz, C_PROJ), s_new


def _order_key(score):
    bits = pltpu.bitcast(score, I32)
    return jnp.where(bits < 0, (bits ^ 0x7FFFFFFF) + 1, bits)


KC = 256
KEY_CLASSES = 8


def _dsa_prompt_block(nk, qb, iq_t, w_t, q_t, kbf_s, vt_s, ikbf_s, key_s, bias_s, ot_s):
    ik = ikbf_s[0:nk, :]
    acc = jnp.zeros((nk, Q_BLOCK), F32)
    for hp in range(IDX_HEADS // 2):
        h0, h1 = 2 * hp, 2 * hp + 1
        iq_pair = jnp.concatenate([iq_t[h0 * IDX_DIM:(h0 + 1) * IDX_DIM, :],
                                   iq_t[h1 * IDX_DIM:(h1 + 1) * IDX_DIM, :]], axis=1)
        st = jnp.dot(ik, iq_pair, preferred_element_type=F32)
        acc = acc + jnp.maximum(st[:, :Q_BLOCK], 0.0) * w_t[h0:h0 + 1, :]
        acc = acc + jnp.maximum(st[:, Q_BLOCK:], 0.0) * w_t[h1:h1 + 1, :]
    visible = _iota((nk, Q_BLOCK), 0) <= qb * Q_BLOCK + _iota((nk, Q_BLOCK), 1)
    key_s[0:nk, :] = _order_key(jnp.where(visible, acc, NEG_INF))

    def count(pred):
        return _reduce_rows(jnp.where(pred(key_s[0:nk, :]), 1.0, 0.0), jnp.sum)

    def bit_body(i, t):
        cand = t + jnp.left_shift(jnp.int32(1), 31 - i)
        return jnp.where(count(lambda kk: kk >= cand) >= TOPK, cand, t)

    thr = lax.fori_loop(0, 32, bit_body, jnp.full((1, Q_BLOCK), INT_MIN, I32))
    need = TOPK - count(lambda kk: kk > thr)
    n_eq = count(lambda kk: kk == thr)
    keys = key_s[0:nk, :]
    floor = jnp.maximum(thr, KEY_NEG_INF + 1)
    bias_s[0:nk, :] = jnp.where((keys >= floor) & (keys < KEY_POS_INF), 0.0, NEG_INF)
    has_tie = jnp.max(jnp.where((n_eq > need) & (thr > KEY_NEG_INF), 1.0, 0.0)) > 0.0

    @pl.when(has_tie)
    def _():
        tri = (_iota((KC, KC), 0) >= _iota((KC, KC), 1)).astype(BF16)
        run = jnp.zeros((1, Q_BLOCK), F32)
        for cc in range(nk // KC):
            kk = key_s[cc * KC:(cc + 1) * KC, :]
            eq = kk == thr
            eqf = jnp.where(eq, 1.0, 0.0)
            rank = jnp.dot(tri, eqf.astype(BF16), preferred_element_type=F32) + run
            ok = ((kk > thr) | (eq & (rank <= need))) & (kk > KEY_NEG_INF) & (kk < KEY_POS_INF)
            bias_s[cc * KC:(cc + 1) * KC, :] = jnp.where(ok, 0.0, NEG_INF)
            run = run + jnp.sum(eqf, axis=0, keepdims=True)

    b = bias_s[0:nk, :]
    bias2 = jnp.concatenate([b, b], axis=1)
    for g in range(B_KV_HEADS):
        pair = jnp.concatenate([q_t[(2 * g) * HEAD_DIM:(2 * g + 1) * HEAD_DIM, :],
                                q_t[(2 * g + 1) * HEAD_DIM:(2 * g + 2) * HEAD_DIM, :]], axis=1)
        s = jnp.dot(kbf_s[g, 0:nk, :], pair, preferred_element_type=F32) + bias2
        mx = _reduce_rows(s, jnp.max)
        pr = jnp.exp((s - mx).astype(BF16))
        acc = jnp.dot(vt_s[g, :, 0:nk], pr, preferred_element_type=F32)
        o = acc[0:HEAD_DIM] / acc[HEAD_DIM:HEAD_DIM + 1]
        ot_s[(2 * g) * HEAD_DIM:(2 * g + 1) * HEAD_DIM, :] = o[:, :Q_BLOCK]
        ot_s[(2 * g + 1) * HEAD_DIM:(2 * g + 2) * HEAD_DIM, :] = o[:, Q_BLOCK:]


def _dsa_prompt_kernel(q_ref, iq_ref, ikwq_ref, ikw_ref, k_ref, v_ref, y_ref,
                       kbf_s, vt_s, ikbf_s, key_s, bias_s, ot_s, *, T, classes):
    qb = pl.program_id(1)

    @pl.when(qb == 0)
    def _():
        ikbf_s[...] = ikw_ref[0][:, :IDX_DIM].astype(BF16)
        vt_s[...] = jnp.ones(vt_s.shape, BF16)
        for g in range(B_KV_HEADS):
            kbf_s[g] = k_ref[0][:, g * HEAD_DIM:(g + 1) * HEAD_DIM].astype(BF16)
        for cc in range(T // KC):
            vt = v_ref[0, cc * KC:(cc + 1) * KC, :].T.astype(BF16)
            for g in range(B_KV_HEADS):
                vt_s[g, 0:HEAD_DIM, cc * KC:(cc + 1) * KC] = vt[g * HEAD_DIM:(g + 1) * HEAD_DIM]

    iq_t = iq_ref[0].T.astype(BF16)
    w_t = ikwq_ref[0].T[IDX_DIM:IDX_DIM + IDX_HEADS, :] * (IDX_DIM ** -0.5 * IDX_HEADS ** -0.5)
    q_t = (q_ref[0] * (HEAD_DIM ** -0.5)).T.astype(BF16)

    per_class = (T // Q_BLOCK) // classes
    for cls in range(classes):
        nk = (cls + 1) * per_class * Q_BLOCK

        @pl.when(qb // per_class == cls)
        def _(nk=nk):
            _dsa_prompt_block(nk, qb, iq_t, w_t, q_t, kbf_s, vt_s, ikbf_s, key_s, bias_s, ot_s)

    y_ref[0] = ot_s[...].T


def _dsa_prompt(q3, iq3, ikw3, k3, v3):
    bsz, t, _ = q3.shape
    nqb = t // Q_BLOCK
    blk = lambda b, j: (b, j, 0)
    full = lambda b, j: (b, 0, 0)
    classes = math.gcd(nqb, KEY_CLASSES)
    assert (nqb // classes) * Q_BLOCK % KC == 0
    kern = functools.partial(_dsa_prompt_kernel, T=t, classes=classes)
    return pl.pallas_call(
        kern,
        grid=(bsz, nqb),
        in_specs=[pl.BlockSpec((1, Q_BLOCK, B_WIDTH), blk),
                  pl.BlockSpec((1, Q_BLOCK, IQ_WIDTH), blk),
                  pl.BlockSpec((1, Q_BLOCK, LANES), blk),
                  pl.BlockSpec((1, t, LANES), full),
                  pl.BlockSpec((1, t, KV_WIDTH), full),
                  pl.BlockSpec((1, t, KV_WIDTH), full)],
        out_specs=pl.BlockSpec((1, Q_BLOCK, B_WIDTH), blk),
        out_shape=jax.ShapeDtypeStruct((bsz, t, B_WIDTH), F32),
        scratch_shapes=[pltpu.VMEM((B_KV_HEADS, t, HEAD_DIM), BF16),
                        pltpu.VMEM((B_KV_HEADS, HEAD_DIM + BF16_ROWS, t), BF16),
                        pltpu.VMEM((t, IDX_DIM), BF16),
                        pltpu.VMEM((t, Q_BLOCK), I32),
                        pltpu.VMEM((t, Q_BLOCK), F32),
                        pltpu.VMEM((B_WIDTH, Q_BLOCK), F32)],
        compiler_params=_cparams("parallel", "arbitrary"),
    )(q3, iq3, ikw3, ikw3, k3, v3)


PG_IDX = 64
PG_ATT = 32


def _page_pipeline(pt_ref, base, steps, arrays):
    b = pl.program_id(0)
    j = pl.program_id(1)
    n = b * steps + j
    slot = n % 2
    pg = arrays[0][1].shape[1]

    def copies(bq, jq, sl):
        return [pltpu.make_async_copy(hbm.at[base + pt_ref[bq, jq * pg + i]], buf.at[sl, i], sem.at[sl])
                for hbm, buf, sem in arrays for i in range(pg)]

    @pl.when(n == 0)
    def _():
        for cp in copies(b, j, slot):
            cp.start()

    wrap = j + 1 == steps

    @pl.when(n + 1 < pl.num_programs(0) * steps)
    def _():
        for cp in copies(jnp.where(wrap, b + 1, b), jnp.where(wrap, 0, j + 1), 1 - slot):
            cp.start()

    for cp in copies(b, j, slot):
        cp.wait()
    return slot


def _sidx_kernel(pt_ref, iqf_ref, wcol_ref, pages_hbm, out_ref, buf, sem, *, Ts, base, steps):
    slot = _page_pipeline(pt_ref, base, steps, [(pages_hbm, buf, sem)])
    iqf = iqf_ref[0].astype(BF16)
    wcol = wcol_ref[0]
    for i in range(buf.shape[1]):
        s = _dot(iqf, buf[slot, i])
        r = jnp.maximum(s, 0.0) * wcol
        out_ref[0, i] = jnp.sum(r.reshape(IDX_HEADS, Ts, PAGE), axis=0)


def _sidx(page_table, iqf, wcol, cache_idx_kt, base, Ts):
    db, n_pages = page_table.shape
    PG = math.gcd(n_pages, PG_IDX)
    steps = n_pages // PG
    grid_spec = pltpu.PrefetchScalarGridSpec(
        num_scalar_prefetch=1,
        grid=(db, steps),
        in_specs=[pl.BlockSpec((1, IDX_HEADS * Ts, IDX_DIM), lambda b, j, pt: (b, 0, 0)),
                  pl.BlockSpec((1, IDX_HEADS * Ts, 1), lambda b, j, pt: (b, 0, 0)),
                  pl.BlockSpec(memory_space=pl.ANY)],
        out_specs=pl.BlockSpec((1, PG, Ts, PAGE), lambda b, j, pt: (b, j, 0, 0)),
        scratch_shapes=[pltpu.VMEM((2, PG, IDX_DIM, PAGE), F32), pltpu.SemaphoreType.DMA((2,))],
    )
    return pl.pallas_call(
        functools.partial(_sidx_kernel, Ts=Ts, base=base, steps=steps),
        grid_spec=grid_spec,
        out_shape=jax.ShapeDtypeStruct((db, n_pages, Ts, PAGE), F32),
        compiler_params=_cparams("arbitrary", "arbitrary"),
    )(page_table, iqf, wcol, cache_idx_kt)


def _ssel_kernel(sc_ref, iqf_ref, wcol_ref, iknew_ref, bias_ref, key_s, *, Ts, NT, NB):
    vis = _iota((Ts, PAGE), 1) <= _iota((Ts, PAGE), 0)
    for bb in range(NB):
        key_s[bb, 0:NT] = _order_key(sc_ref[bb])
        s = _dot_nt(iqf_ref[bb], iknew_ref[bb])
        r = jnp.maximum(s, 0.0) * wcol_ref[bb]
        snew = jnp.sum(r.reshape(IDX_HEADS, Ts, PAGE), axis=0)
        key_s[bb, NT] = _order_key(jnp.where(vis, snew, NEG_INF))

    def count(mask):
        ones = jnp.where(mask, 1.0, 0.0)
        part = ones[NT]
        if NT % 8 == 0:
            part = part + jnp.sum(jnp.sum(ones[0:NT].reshape(NT // 8, 8, Ts, PAGE), axis=0), axis=0)
        else:
            part = part + jnp.sum(ones[0:NT], axis=0)
        return jnp.sum(part, axis=1, keepdims=True)

    def bit_body(i, ts):
        out = []
        for bb in range(NB):
            cand = ts[bb] + jnp.left_shift(jnp.int32(1), 31 - i)
            cnt = count(key_s[bb] >= cand[None])
            out.append(jnp.where(cnt >= TOPK, cand, ts[bb]))
        return tuple(out)

    thrs = lax.fori_loop(0, 32, bit_body, tuple(jnp.full((Ts, 1), INT_MIN, I32) for _ in range(NB)))
    for bb in range(NB):
        thr = thrs[bb]
        keys = key_s[bb]
        need = TOPK - count(keys > thr[None])
        n_eq = count(keys == thr[None])
        finite = (keys > KEY_NEG_INF) & (keys < KEY_POS_INF)
        bias_ref[bb] = jnp.where((keys >= thr[None]) & finite, 0.0, NEG_INF)
        has_tie = jnp.max(jnp.where((n_eq > need) & (thr > KEY_NEG_INF), 1.0, 0.0)) > 0.0

        @pl.when(has_tie)
        def _(bb=bb, thr=thr, need=need):
            triu = (_iota((PAGE, PAGE), 0) <= _iota((PAGE, PAGE), 1)).astype(BF16)

            def tile_body(j, run):
                kk = key_s[bb, j]
                eq = kk == thr
                eqf = jnp.where(eq, 1.0, 0.0)
                rank = jnp.dot(eqf.astype(BF16), triu, preferred_element_type=F32) + run
                sel = (kk > thr) | (eq & (rank <= need))
                ok = sel & (kk > KEY_NEG_INF) & (kk < KEY_POS_INF)
                bias_ref[bb, j] = jnp.where(ok, 0.0, NEG_INF)
                return run + jnp.sum(eqf, axis=1, keepdims=True)

            lax.fori_loop(0, NT + 1, tile_body, jnp.zeros((Ts, 1), F32))


def _ssel(scores, iqf, wcol, iknew):
    db, nt, ts, _ = scores.shape
    NB = math.gcd(db, SEQ_ROWS_PER_STEP)
    b3 = lambda b: (b, 0, 0)
    b4 = lambda b: (b, 0, 0, 0)
    return pl.pallas_call(
        functools.partial(_ssel_kernel, Ts=ts, NT=nt, NB=NB),
        grid=(db // NB,),
        in_specs=[pl.BlockSpec((NB, nt, ts, PAGE), b4),
                  pl.BlockSpec((NB, IDX_HEADS * ts, IDX_DIM), b3),
                  pl.BlockSpec((NB, IDX_HEADS * ts, 1), b3),
                  pl.BlockSpec((NB, PAGE, IDX_DIM), b3)],
        out_specs=pl.BlockSpec((NB, nt + 1, ts, PAGE), b4),
        out_shape=jax.ShapeDtypeStruct((db, nt + 1, ts, PAGE), F32),
        scratch_shapes=[pltpu.VMEM((NB, nt + 1, ts, PAGE), I32)],
        compiler_params=_cparams("parallel"),
    )(scores, iqf, wcol, iknew)


def _sattn_kernel(pt_ref, qbd_ref, bias_ref, biasn_ref, knew_ref, vnew_ref, k_hbm, v_hbm, y_ref,
                  m_s, l_s, acc_s, kbuf, vbuf, ksem, vsem, *, Ts, base, steps):
    slot = _page_pipeline(pt_ref, base, steps, [(k_hbm, kbuf, ksem), (v_hbm, vbuf, vsem)])
    j = pl.program_id(1)

    @pl.when(j == 0)
    def _():
        m_s[...] = jnp.full(m_s.shape, NEG_INF, F32)
        l_s[...] = jnp.zeros(l_s.shape, F32)
        acc_s[...] = jnp.zeros(acc_s.shape, F32)

    qbd = qbd_ref[0].astype(BF16)

    def rows(b):
        return jnp.concatenate([b] * B_HEADS, axis=0)

    def update(s, pv):
        m_old = m_s[...]
        m_new = jnp.maximum(m_old, jnp.max(s, axis=1, keepdims=True))
        m_safe = jnp.where(m_new == NEG_INF, 0.0, m_new)
        scale = jnp.exp(m_old - m_safe)
        pr = jnp.exp(s - m_safe)
        l_s[...] = scale * l_s[...] + jnp.sum(pr, axis=1, keepdims=True)
        acc_s[...] = scale * acc_s[...] + pv(pr)
        m_s[...] = m_new

    s = jnp.concatenate(
        [_dot(qbd, kbuf[slot, i]) + rows(bias_ref[0, i]) for i in range(kbuf.shape[1])],
        axis=1)

    def pv(pr):
        acc = jnp.zeros((B_HEADS * Ts, KV_WIDTH), F32)
        for i in range(vbuf.shape[1]):
            acc = acc + _dot_nt(pr[:, i * PAGE:(i + 1) * PAGE], vbuf[slot, i])
        return acc

    update(s, pv)

    @pl.when(j == pl.num_programs(1) - 1)
    def _():
        sn = _dot(qbd, knew_ref[0]) + rows(biasn_ref[0, 0])
        update(sn, lambda pr: _dot_nt(pr, vnew_ref[0]))
        o = acc_s[...] / l_s[...]
        for h in range(B_HEADS):
            g = h // (B_HEADS // B_KV_HEADS)
            y_ref[0, :, h * HEAD_DIM:(h + 1) * HEAD_DIM] = o[h * Ts:(h + 1) * Ts, g * HEAD_DIM:(g + 1) * HEAD_DIM]


def _sattn(page_table, qbd, bias, knew_t, vnew_t, cache_kt, cache_vt, base, Ts):
    db, n_pages = page_table.shape
    PG = math.gcd(n_pages, PG_ATT)
    steps = n_pages // PG
    b3 = lambda b, j, pt: (b, 0, 0)
    nrows = B_HEADS * Ts
    grid_spec = pltpu.PrefetchScalarGridSpec(
        num_scalar_prefetch=1,
        grid=(db, steps),
        in_specs=[pl.BlockSpec((1, nrows, KV_WIDTH), b3),
                  pl.BlockSpec((1, PG, Ts, PAGE), lambda b, j, pt: (b, j, 0, 0)),
                  pl.BlockSpec((1, 1, Ts, PAGE), lambda b, j, pt: (b, n_pages, 0, 0)),
                  pl.BlockSpec((1, KV_WIDTH, PAGE), b3),
                  pl.BlockSpec((1, KV_WIDTH, PAGE), b3),
                  pl.BlockSpec(memory_space=pl.ANY),
                  pl.BlockSpec(memory_space=pl.ANY)],
        out_specs=pl.BlockSpec((1, Ts, B_WIDTH), b3),
        scratch_shapes=[pltpu.VMEM((nrows, 1), F32),
                        pltpu.VMEM((nrows, 1), F32),
                        pltpu.VMEM((nrows, KV_WIDTH), F32),
                        pltpu.VMEM((2, PG, KV_WIDTH, PAGE), F32),
                        pltpu.VMEM((2, PG, KV_WIDTH, PAGE), F32),
                        pltpu.SemaphoreType.DMA((2,)),
                        pltpu.SemaphoreType.DMA((2,))],
    )
    return pl.pallas_call(
        functools.partial(_sattn_kernel, Ts=Ts, base=base, steps=steps),
        grid_spec=grid_spec,
        out_shape=jax.ShapeDtypeStruct((db, Ts, B_WIDTH), F32),
        compiler_params=_cparams("arbitrary", "arbitrary"),
    )(page_table, qbd, bias, bias, knew_t, vnew_t, cache_kt, cache_vt)


def _dsa_sample(q3, iq3, ikw3, k3, v3, cache_kt, cache_vt, cache_idx_kt, page_table, base):
    db, ts, _ = q3.shape
    iqf = iq3.reshape(db, ts, IDX_HEADS, IDX_DIM).transpose(0, 2, 1, 3).reshape(db, IDX_HEADS * ts, IDX_DIM)
    iw = ikw3[:, :, IDX_DIM:IDX_DIM + IDX_HEADS] * (IDX_DIM ** -0.5 * IDX_HEADS ** -0.5)
    wcol = iw.transpose(0, 2, 1).reshape(db, IDX_HEADS * ts, 1)
    qh = (q3 * (HEAD_DIM ** -0.5)).reshape(db, ts, B_HEADS, HEAD_DIM).transpose(0, 2, 1, 3)
    own = (jnp.arange(B_HEADS)[:, None] // (B_HEADS // B_KV_HEADS)) == jnp.arange(B_KV_HEADS)[None, :]
    qbd = jnp.where(own[None, :, None, :, None], qh[:, :, :, None, :], 0.0)
    qbd = qbd.reshape(db, B_HEADS * ts, KV_WIDTH)
    pad = lambda a: jnp.pad(a, ((0, 0), (0, PAGE - ts), (0, 0)))
    iknew = pad(ikw3[:, :, :IDX_DIM])
    scores = _sidx(page_table, iqf, wcol, cache_idx_kt, base, ts)
    bias = _ssel(scores, iqf, wcol, iknew)
    knew_t = pad(k3).transpose(0, 2, 1)
    vnew_t = pad(v3).transpose(0, 2, 1)
    return _sattn(page_table, qbd, bias, knew_t, vnew_t, cache_kt, cache_vt, base, ts)


def _ffn_kernel(x_ref, ya_ref, yb_ref, yc_ref, wo_ref, gf_ref, wg_ref, wu_ref, wd_ref, gfin_ref,
                o_ref, *, final):
    mixed = (jnp.dot(ya_ref[...].astype(BF16), wo_ref[0:A_WIDTH, :], preferred_element_type=F32)
             + jnp.dot(yb_ref[...].astype(BF16), wo_ref[A_WIDTH:A_WIDTH + B_WIDTH, :],
                       preferred_element_type=F32)
             + jnp.dot(yc_ref[...].astype(BF16), wo_ref[A_WIDTH + B_WIDTH:, :],
                       preferred_element_type=F32))
    x1 = x_ref[...] + mixed
    ms = jnp.mean(x1 * x1, axis=-1, keepdims=True)
    hf = (x1 * lax.rsqrt(ms + NORM_EPS) * gf_ref[...]).astype(BF16)
    gate = jnp.dot(hf, wg_ref[...], preferred_element_type=F32)
    up = jnp.dot(hf, wu_ref[...], preferred_element_type=F32)
    act = (gate * _sigmoid(gate) * up).astype(BF16)
    x2 = x1 + jnp.dot(act, wd_ref[...], preferred_element_type=F32)
    if final:
        ms2 = jnp.mean(x2 * x2, axis=-1, keepdims=True)
        x2 = x2 * lax.rsqrt(ms2 + NORM_EPS) * gfin_ref[...]
    o_ref[...] = x2


def _ffn(x2d, ya, yb, yc, wo, gf, wg, wu, wd, gfin, final, tm):
    n, d = x2d.shape
    dff = wg.shape[1]
    row = lambda i: (i, 0)
    fixed = lambda i: (0, 0)
    once = dict(pipeline_mode=pl.Buffered(1))
    return pl.pallas_call(
        functools.partial(_ffn_kernel, final=final),
        grid=(n // tm,),
        in_specs=[pl.BlockSpec((tm, d), row), pl.BlockSpec((tm, A_WIDTH), row),
                  pl.BlockSpec((tm, B_WIDTH), row), pl.BlockSpec((tm, C_WIDTH), row),
                  pl.BlockSpec((d, d), fixed, **once), pl.BlockSpec((1, d), fixed),
                  pl.BlockSpec((d, dff), fixed, **once), pl.BlockSpec((d, dff), fixed, **once),
                  pl.BlockSpec((dff, d), fixed, **once), pl.BlockSpec((1, d), fixed)],
        out_specs=pl.BlockSpec((tm, d), row),
        out_shape=jax.ShapeDtypeStruct((n, d), F32),
        compiler_params=_cparams("parallel"),
    )(x2d, ya, yb, yc, wo, gf, wg, wu, wd, gfin)


def _rope_tables(pos):
    half = ROT_DIM // 2
    inv = ROPE_THETA ** (-jnp.arange(half, dtype=F32) / half)
    ang = pos.astype(F32)[:, None] * inv[None, :]
    j = jnp.arange(LANES) % HEAD_DIM
    cos = jnp.cos(ang)[:, j % half]
    sin = jnp.sin(ang)[:, j % half]
    cos_t = jnp.where(j[None, :] < ROT_DIM, cos, 1.0)
    sina = jnp.where(j[None, :] < half, -sin, 0.0)
    sinb = jnp.where((j[None, :] >= half) & (j[None, :] < ROT_DIM), sin, 0.0)
    return cos_t, sina, sinb


def _pad_w_in(w):
    d = w.shape[0]
    z = lambda n: jnp.zeros((d, n), w.dtype)
    gates = jnp.repeat(w[:, 4 * A_WIDTH:A_PROJ], HEAD_DIM, axis=1)
    return jnp.concatenate([w[:, :4 * A_WIDTH], gates,
                            w[:, A_PROJ:A_PROJ + B_PROJ], z(B_PAD - B_PROJ),
                            w[:, A_PROJ + B_PROJ:]], axis=1).astype(BF16)


def _layer(x3, l, tabs, mstate, shift0, s0, attn_fn, p, wts, final, tm):
    bsz, t, d = x3.shape
    n = bsz * t
    pa, q, k, v, iq, ikw, pc = _inproj(x3.reshape(n, d), p['norm_mix'][l].reshape(1, d), wts['w_in'][l], tabs, tm)
    r3 = lambda a: a.reshape(bsz, t, a.shape[-1])
    y_a, c_new, n_new, m_new = _mlstm(r3(pa), *mstate, p['mlstm_gate_b'][l], p['mlstm_norm'][l])
    y_b = attn_fn(r3(q), r3(iq), r3(ikw), r3(k), r3(v))
    y_c, shift, s_new = _rwkv(r3(pc), shift0, s0, p, l)
    x_new = _ffn(x3.reshape(n, d), y_a.reshape(n, -1), y_b.reshape(n, -1), y_c.reshape(n, -1),
                 wts['w_out'][l], p['norm_ffn'][l].reshape(1, d), wts['w_gate'][l], wts['w_up'][l],
                 wts['w_down'][l], p['norm_final'].reshape(1, d), final, tm)
    new = (k.reshape(bsz, t, B_KV_HEADS, HEAD_DIM), v.reshape(bsz, t, B_KV_HEADS, HEAD_DIM),
           r3(ikw)[:, :, :IDX_DIM], c_new, n_new, m_new, shift, s_new)
    return x_new.reshape(bsz, t, d), new


def kernel(x_prompt, x_sample, cache_k, cache_v, cache_idx_k, state_mlstm_C, state_mlstm_n, state_mlstm_m, state_rwkv_shift, state_rwkv_S, page_table, norm_mix, w_in, mlstm_gate_b, mlstm_norm, rwkv_mu, rwkv_w0, rwkv_w2, rwkv_a0, rwkv_a2, rwkv_g2, rwkv_k_k, rwkv_k_a, rwkv_r_k, rwkv_ln_w, rwkv_ln_b, w_out, norm_ffn, w_gate, w_up, w_down, norm_final):
    p = dict(norm_mix=norm_mix, mlstm_gate_b=mlstm_gate_b, mlstm_norm=mlstm_norm,
             rwkv_mu=rwkv_mu, rwkv_w0=rwkv_w0, rwkv_w2=rwkv_w2, rwkv_a0=rwkv_a0, rwkv_a2=rwkv_a2,
             rwkv_g2=rwkv_g2, rwkv_k_k=rwkv_k_k, rwkv_k_a=rwkv_k_a, rwkv_r_k=rwkv_r_k,
             rwkv_ln_w=rwkv_ln_w, rwkv_ln_b=rwkv_ln_b, norm_ffn=norm_ffn, norm_final=norm_final)
    depth = w_in.shape[0]
    wts = dict(w_in=[_pad_w_in(w_in[l]) for l in range(depth)],
               w_out=w_out.astype(BF16), w_gate=w_gate.astype(BF16),
               w_up=w_up.astype(BF16), w_down=w_down.astype(BF16))
    bp, tp, _ = x_prompt.shape
    db, ts, _ = x_sample.shape
    past = page_table.shape[1] * PAGE
    tabs_p = _rope_tables(jnp.arange(tp, dtype=I32))
    tabs_s = tuple(jnp.tile(a, (db, 1)) for a in _rope_tables(past + jnp.arange(ts, dtype=I32)))
    tm_p = min(512, bp * tp)
    tm_s = db * ts
    zc = jnp.zeros((bp, A_HEADS, HEAD_DIM, HEAD_DIM), F32)
    zn = jnp.zeros((bp, A_HEADS, HEAD_DIM), F32)
    zm = jnp.zeros((bp, A_HEADS), F32)
    zshift = jnp.zeros((bp, C_PROJ), F32)
    zs = jnp.zeros((bp, C_HEADS, HEAD_DIM, HEAD_DIM), F32)
    n_pool = cache_k.shape[1]
    cache_kt = cache_k.transpose(0, 1, 3, 4, 2).reshape(depth * n_pool, KV_WIDTH, PAGE)
    cache_vt = cache_v.transpose(0, 1, 3, 4, 2).reshape(depth * n_pool, KV_WIDTH, PAGE)
    cache_idx_kt = cache_idx_k.transpose(0, 1, 3, 2).reshape(depth * n_pool, IDX_DIM, PAGE)
    xp, xs = x_prompt, x_sample
    new_p, new_s = [], []
    for l in range(depth):
        final = l == depth - 1
        xp, st = _layer(xp, l, tabs_p, (zc, zn, zm), zshift, zs, _dsa_prompt, p, wts, final, tm_p)
        new_p.append(st)
        attn_s = functools.partial(_dsa_sample, cache_kt=cache_kt, cache_vt=cache_vt,
                                   cache_idx_kt=cache_idx_kt, page_table=page_table, base=l * n_pool)
        xs, st = _layer(xs, l, tabs_s, (state_mlstm_C[l], state_mlstm_n[l], state_mlstm_m[l]),
                        state_rwkv_shift[l], state_rwkv_S[l], attn_s, p, wts, final, tm_s)
        new_s.append(st)
    stack = lambda states, i: jnp.stack([st[i] for st in states])
    outs_p = [stack(new_p, i) for i in range(8)]
    outs_s = [stack(new_s, i) for i in range(8)]
    return (xp, xs, *outs_p, *outs_s)
```
